```python
import jax, jax.numpy as jnp
from jax import lax
import numpy as np

D_MODEL = 2048
BATCH = 2
SEQ = 16384
DEPTH = 1

CHUNK = 64
PE_DIM = 256
HG_DK = 128
HG_DV = 128
HG_HEADS = D_MODEL // HG_DK
HG_KW = HG_HEADS * HG_DK
HG_VW = HG_HEADS * HG_DV
SB_DH = 128
SB_HEADS = 8
SB_W = SB_HEADS * SB_DH
Q_BLOCK = 128
LN_EPS = 1e-5
DN_ALPHA = (2 * DEPTH) ** 0.25
DN_BETA = (8 * DEPTH) ** -0.25
IN_SPLITS = (HG_KW, HG_KW, HG_VW, HG_VW, SB_W, SB_W, SB_W, SB_W)
IN_COLS = int(sum(IN_SPLITS))
IN_OFFSETS = tuple(int(o) for o in np.cumsum(IN_SPLITS)[:-1])

kernel_name = "hgrn2_stickbreak_gated_hybrid"


def layer_norm(x, g, b):
    xf = x.astype(jnp.float32)
    mu = jnp.mean(xf, axis=-1, keepdims=True)
    var = jnp.mean(jnp.square(xf - mu), axis=-1, keepdims=True)
    return ((xf - mu) * lax.rsqrt(var + LN_EPS) * g + b).astype(x.dtype)


def group_rms_norm(y, g):
    yf = y.astype(jnp.float32)
    yf = yf * lax.rsqrt(jnp.mean(jnp.square(yf), axis=-1, keepdims=True) + LN_EPS)
    Bn, T, H, d = y.shape
    return yf.reshape(Bn, T, H * d) * g


def hgrn2(q, f_logit, i_in, lb):
    f32 = jnp.float32
    Bn, T, H, DK = q.shape
    DV = i_in.shape[-1]
    nc = T // CHUNK
    lbh = lb.reshape(H, DK)
    f = lbh + (1.0 - lbh) * jax.nn.sigmoid(f_logit.astype(f32))
    logf = jnp.log(f)
    k = 1.0 - f

    def chunks(a):
        return a.reshape(Bn, nc, CHUNK, H, a.shape[-1]).transpose(1, 0, 3, 2, 4)

    qc = chunks(q.astype(f32))
    kc = chunks(k)
    vc = chunks(i_in.astype(f32))
    bc = jnp.cumsum(chunks(logf), axis=3)
    b_last = bc[:, :, :, -1:, :]
    q_dec = qc * jnp.exp(bc)
    k_inv = kc * jnp.exp(-bc)
    k_end = kc * jnp.exp(b_last - bc)
    decay = jnp.exp(b_last[:, :, :, 0, :])
    causal = jnp.tril(jnp.ones((CHUNK, CHUNK), dtype=bool))

    def step(S, xs):
        qd, ki, ke, v, dec = xs
        a = jnp.where(causal, jnp.einsum('bhtk,bhsk->bhts', qd, ki), 0.0)
        o = jnp.einsum('bhts,bhsv->bhtv', a, v) + jnp.einsum('bhtk,bhkv->bhtv', qd, S)
        S = dec[..., None] * S + jnp.einsum('bhsk,bhsv->bhkv', ke, v)
        return S, o

    S0 = jnp.zeros((Bn, H, DK, DV), f32)
    _, o = lax.scan(step, S0, (q_dec, k_inv, k_end, vc, decay))
    return o.transpose(1, 0, 3, 2, 4).reshape(Bn, T, H, DV)


def stick_breaking(q, k, v):
    f32 = jnp.float32
    Bn, T, H, D = q.shape
    scale = D ** -0.5
    nb = T // Q_BLOCK

    def blocks(a):
        return a.reshape(Bn, nb, Q_BLOCK, H, D).transpose(1, 0, 3, 2, 4)

    qb, kb, vb = blocks(q), blocks(k), blocks(v)
    idx = jnp.arange(Q_BLOCK)
    strict = idx[None, :] < idx[:, None]
    later = (idx[:, None] > idx[None, :]).astype(f32)
    acc = jnp.zeros((nb, Bn, H, Q_BLOCK, D), f32)
    R = jnp.zeros((nb, Bn, H, Q_BLOCK), f32)
    for d in range(nb):
        n = nb - d
        z = jnp.einsum('nbhqd,nbhkd->nbhqk', qb[d:], kb[:n]).astype(f32) * scale
        l = jax.nn.log_sigmoid(-z)
        if d == 0:
            l = jnp.where(strict, l, 0.0)
        suffix = jnp.einsum('nbhqj,js->nbhqs', l, later)
        A = jnp.exp(jax.nn.log_sigmoid(z) + suffix + R[d:, ..., None])
        if d == 0:
            A = jnp.where(strict, A, 0.0)
        acc = acc.at[d:].add(jnp.einsum('nbhqk,nbhkd->nbhqd', A.astype(vb.dtype), vb[:n]).astype(f32))
        R = R.at[d:].add(jnp.sum(l, axis=-1))
    return acc.transpose(1, 0, 3, 2, 4).reshape(Bn, T, H, D).astype(q.dtype)


def setup_inputs(seed: int = 0) -> dict:
    key = jax.random.key(seed)
    ks = jax.random.split(key, 18)
    f32 = jnp.float32

    def nrm(k, shape, s):
        return jax.random.normal(k, shape, f32) * s

    return {
        "x": nrm(ks[0], (BATCH, SEQ, D_MODEL), 1.0),
        "p": nrm(ks[1], (DEPTH, BATCH, SEQ, PE_DIM), 1.0),
        "ln_in_g": 1.0 + nrm(ks[2], (D_MODEL,), 0.02),
        "ln_in_b": nrm(ks[3], (D_MODEL,), 0.02),
        "w_in": nrm(ks[4], (DEPTH, D_MODEL, IN_COLS), D_MODEL ** -0.5),
        "hg_lb_logits": nrm(ks[5], (DEPTH + 1, HG_KW), 0.1),
        "hg_norm_g": 1.0 + nrm(ks[6], (DEPTH, HG_VW), 0.02),
        "w_merge": nrm(ks[7], (DEPTH, D_MODEL, 2 * D_MODEL), D_MODEL ** -0.5),
        "b_merge": nrm(ks[8], (DEPTH, 2 * D_MODEL), 0.02),
        "w_br_hg": nrm(ks[9], (DEPTH, HG_VW, D_MODEL), HG_VW ** -0.5),
        "w_br_sb": nrm(ks[10], (DEPTH, SB_W, D_MODEL), SB_W ** -0.5),
        "w_out": nrm(ks[11], (DEPTH, D_MODEL, D_MODEL), DN_BETA * D_MODEL ** -0.5),
        "w_pe": nrm(ks[12], (DEPTH, PE_DIM, D_MODEL), DN_BETA * PE_DIM ** -0.5),
        "w_pg": nrm(ks[13], (DEPTH, D_MODEL, D_MODEL), D_MODEL ** -0.5),
        "b_pg": nrm(ks[14], (DEPTH, D_MODEL), 0.02),
        "ln_g": 1.0 + nrm(ks[15], (DEPTH, D_MODEL), 0.02),
        "ln_b": nrm(ks[16], (DEPTH, D_MODEL), 0.02),
    }


def reference(x, p, ln_in_g, ln_in_b, w_in, hg_lb_logits, hg_norm_g, w_merge, b_merge,
              w_br_hg, w_br_sb, w_out, w_pe, w_pg, b_pg, ln_g, ln_b):
    Bn, T, _ = x.shape
    h = layer_norm(x, ln_in_g, ln_in_b)
    lb_all = jnp.cumsum(jax.nn.softmax(hg_lb_logits.astype(jnp.float32), axis=0), axis=0)
    for i in range(DEPTH):
        proj = h @ w_in[i]
        hq, hf, hi, hg, sq, sk, sv, sg = jnp.split(proj, IN_OFFSETS, axis=-1)
        oa = hgrn2(hq.reshape(Bn, T, HG_HEADS, HG_DK), hf.reshape(Bn, T, HG_HEADS, HG_DK),
                   hi.reshape(Bn, T, HG_HEADS, HG_DV), lb_all[i])
        ya = (group_rms_norm(oa, hg_norm_g[i]) * jax.nn.silu(hg.astype(jnp.float32))).astype(h.dtype)
        ob = stick_breaking(sq.reshape(Bn, T, SB_HEADS, SB_DH), sk.reshape(Bn, T, SB_HEADS, SB_DH),
                            sv.reshape(Bn, T, SB_HEADS, SB_DH)).reshape(Bn, T, SB_W)
        yb = ob * jax.nn.silu(sg)
        gates = jax.nn.sigmoid(h @ w_merge[i] + b_merge[i])
        ga, gb = jnp.split(gates, 2, axis=-1)
        m = ga * (ya @ w_br_hg[i]) + gb * (yb @ w_br_sb[i])
        r = DN_ALPHA * h + m @ w_out[i]
        r = r + jax.nn.sigmoid(r @ w_pg[i] + b_pg[i]) * (p[i] @ w_pe[i])
        h = layer_norm(r, ln_g[i], ln_b[i])
    return h
```

```python
import functools

import jax
import jax.numpy as jnp
from jax import lax
from jax.experimental import pallas as pl
from jax.experimental.pallas import tpu as pltpu

LN_EPS = 1e-5
HEAD = 128
HG_CHUNK = 128
HG_HEADS_PER_STEP = 2
HG_ROWS_PER_STEP = 512
SB_BLOCK = 256
VMEM_LIMIT = 52 * 1024 * 1024

F32 = jnp.float32
BF16 = jnp.bfloat16


def _dot(a, b):
    return jnp.dot(a, b, preferred_element_type=F32)


def _dot_nt(a, b):
    return lax.dot_general(a, b, (((1,), (1,)), ((), ())), preferred_element_type=F32)


def _layer_norm(x, g, b):
    mu = jnp.mean(x, axis=-1, keepdims=True)
    xc = x - mu
    var = jnp.mean(xc * xc, axis=-1, keepdims=True)
    return xc * lax.rsqrt(var + LN_EPS) * g + b


def _sigmoid(x):
    return 1.0 / (1.0 + jnp.exp(-x))


def _ln_kernel(x_ref, g_ref, b_ref, o_ref):
    o_ref[...] = _layer_norm(x_ref[...], g_ref[...], b_ref[...]).astype(o_ref.dtype)


def _ln_call(x, g, b, tm=512):
    m, d = x.shape
    return pl.pallas_call(
        _ln_kernel,
        grid=(m // tm,),
        in_specs=[pl.BlockSpec((tm, d), lambda i: (i, 0)),
                  pl.BlockSpec((1, d), lambda i: (0, 0)),
                  pl.BlockSpec((1, d), lambda i: (0, 0))],
        out_specs=pl.BlockSpec((tm, d), lambda i: (i, 0)),
        out_shape=jax.ShapeDtypeStruct((m, d), BF16),
        compiler_params=pltpu.CompilerParams(
            dimension_semantics=("parallel",), vmem_limit_bytes=VMEM_LIMIT),
        name="ln_in",
    )(x, g, b)


def _matmul_kernel(h_ref, w_ref, o_ref):
    o_ref[...] = _dot(h_ref[...], w_ref[...]).astype(o_ref.dtype)


def _matmul_gate_kernel(h_ref, w_ref, b_ref, o_ref):
    o_ref[...] = _sigmoid(_dot(h_ref[...], w_ref[...]) + b_ref[...]).astype(o_ref.dtype)


def _matmul_call(h, w, bias, out_dtype, name, tm=1024, tn=1024):
    m, k = h.shape
    n = w.shape[1]
    tm = min(tm, m)
    in_specs = [pl.BlockSpec((tm, k), lambda i, j: (i, 0)),
                pl.BlockSpec((k, tn), lambda i, j: (0, j))]
    args = [h, w]
    kern = _matmul_kernel
    if bias is not None:
        in_specs.append(pl.BlockSpec((1, tn), lambda i, j: (0, j)))
        args.append(bias)
        kern = _matmul_gate_kernel
    return pl.pallas_call(
        kern,
        grid=(m // tm, n // tn),
        in_specs=in_specs,
        out_specs=pl.BlockSpec((tm, tn), lambda i, j: (i, j)),
        out_shape=jax.ShapeDtypeStruct((m, n), out_dtype),
        compiler_params=pltpu.CompilerParams(
            dimension_semantics=("parallel", "parallel"), vmem_limit_bytes=VMEM_LIMIT),
        name=name,
    )(*args)


def _hgrn2_kernel(q_ref, f_ref, i_ref, g_ref, lbl_ref, ng_ref, o_ref, st_ref):
    c = HG_CHUNK
    n_chunks = q_ref.shape[0] // c
    n_heads = q_ref.shape[1] // HEAD

    @pl.when(pl.program_id(2) == 0)
    def _():
        st_ref[...] = jnp.zeros_like(st_ref)

    lbl = lbl_ref[...]
    e = jnp.exp(lbl - jnp.max(lbl, axis=0, keepdims=True))
    lb_all = e[0:1, :] / jnp.sum(e, axis=0, keepdims=True)

    row = lax.broadcasted_iota(jnp.int32, (c, c), 0)
    col = lax.broadcasted_iota(jnp.int32, (c, c), 1)
    causal = col <= row
    tril = jnp.where(causal, 1.0, 0.0).astype(BF16)
    tril2 = jnp.concatenate([tril, tril], axis=1)

    for hh in range(n_heads):
        cs = slice(hh * HEAD, (hh + 1) * HEAD)
        lb = lb_all[:, cs]
        ng = ng_ref[:, cs]
        for ci in range(n_chunks):
            rs = slice(ci * c, (ci + 1) * c)
            f = lb + (1.0 - lb) * _sigmoid(f_ref[rs, cs].astype(F32))
            logf = jnp.log(f)
            kk = 1.0 - f
            hi = logf.astype(BF16)
            lo = (logf - hi.astype(F32)).astype(BF16)
            bc = _dot(tril2, jnp.concatenate([hi, lo], axis=0))
            b_last = bc[c - 1:c, :]
            b_mid = bc[c // 2 - 1:c // 2, :]
            q = q_ref[rs, cs].astype(F32)
            q_dec = (q * jnp.exp(bc)).astype(BF16)
            q_n = (q * jnp.exp(bc - b_mid)).astype(BF16)
            k_n = (kk * jnp.exp(b_mid - bc)).astype(BF16)
            k_end = (kk * jnp.exp(b_last - bc)).astype(BF16)
            a = jnp.where(causal, _dot_nt(q_n, k_n), 0.0).astype(BF16)
            v_t = i_ref[rs, cs].astype(F32).T.astype(BF16)
            st = st_ref[hh]
            o = _dot_nt(jnp.concatenate([a, q_dec], axis=1),
                        jnp.concatenate([v_t, st.astype(BF16)], axis=1))
            st_ref[hh] = st * jnp.exp(b_last) + _dot(v_t, k_end)
            ms = jnp.mean(o * o, axis=-1, keepdims=True)
            g = g_ref[rs, cs].astype(F32)
            y = o * lax.rsqrt(ms + LN_EPS) * ng * (g * _sigmoid(g))
            o_ref[rs, cs] = y.astype(o_ref.dtype)


def _hgrn2_call(pbf, pf, lb_logits, norm_g, batch, seq, n_heads):
    m = batch * seq
    hb, tt = HG_HEADS_PER_STEP, min(HG_ROWS_PER_STEP, seq)
    w = hb * HEAD
    nt = seq // tt
    n_hg = n_heads // hb
    sec = n_heads * HEAD // w

    def rows(b, h, t):
        return b * nt + t

    return pl.pallas_call(
        _hgrn2_kernel,
        grid=(batch, n_hg, nt),
        in_specs=[pl.BlockSpec((tt, w), lambda b, h, t: (rows(b, h, t), h)),
                  pl.BlockSpec((tt, w), lambda b, h, t: (rows(b, h, t), h)),
                  pl.BlockSpec((tt, w), lambda b, h, t: (rows(b, h, t), sec + h)),
                  pl.BlockSpec((tt, w), lambda b, h, t: (rows(b, h, t), 2 * sec + h)),
                  pl.BlockSpec((2, w), lambda b, h, t: (0, h)),
                  pl.BlockSpec((1, w), lambda b, h, t: (0, h))],
        out_specs=pl.BlockSpec((tt, w), lambda b, h, t: (rows(b, h, t), h)),
        out_shape=jax.ShapeDtypeStruct((m, n_heads * HEAD), BF16),
        scratch_shapes=[pltpu.VMEM((hb, HEAD, HEAD), F32)],
        compiler_params=pltpu.CompilerParams(
            dimension_semantics=("parallel", "parallel", "arbitrary"),
            vmem_limit_bytes=VMEM_LIMIT),
        name="hgrn2",
    )(pbf, pf, pbf, pbf, lb_logits, norm_g)


def _softplus(z):
    return jnp.maximum(z, 0.0) + jnp.log(1.0 + jnp.exp(-jnp.abs(z)))


def _sb_kernel(q_ref, k_ref, v_ref, g_ref, later_ref, o_ref, acc_ref, r_ref):
    blk = SB_BLOCK
    qi = pl.program_id(2)
    q = q_ref[...]
    later2 = later_ref[...]

    def block(kb, diagonal):
        start = pl.multiple_of(kb * blk, blk)
        k = k_ref[pl.ds(start, blk), :]
        v = v_ref[pl.ds(start, blk), :]
        z = _dot_nt(q, k)
        sp = _softplus(z)
        l = -sp
        if diagonal:
            qrow = lax.broadcasted_iota(jnp.int32, (blk, blk), 0)
            kcol = lax.broadcasted_iota(jnp.int32, (blk, blk), 1)
            strict = kcol < qrow
            l = jnp.where(strict, l, 0.0)
        hi = l.astype(BF16)
        lo = (l - hi.astype(F32)).astype(BF16)
        suffix = _dot(jnp.concatenate([hi, lo], axis=1), later2)
        r = r_ref[...]
        log_a = (z - sp) + suffix + jnp.concatenate([r] * (blk // HEAD), axis=1)
        a = jnp.exp(log_a)
        if diagonal:
            a = jnp.where(strict, a, 0.0)
        acc_ref[...] += _dot(a.astype(BF16), v)
        r_ref[...] = r + jnp.sum(l, axis=-1, keepdims=True)

    acc_ref[...] = jnp.zeros_like(acc_ref)
    r_ref[...] = jnp.zeros_like(r_ref)
    block(qi, True)

    def body(j, carry):
        block(qi - j, False)
        return carry

    lax.fori_loop(1, qi + 1, body, 0)

    g = g_ref[...].astype(F32)
    o_ref[...] = (acc_ref[...] * (g * _sigmoid(g))).astype(o_ref.dtype)


def _sb_call(pbf, later2, batch, seq, n_heads, col0):
    m = batch * seq
    blk = SB_BLOCK
    nq = seq // blk
    c0 = col0 // HEAD

    return pl.pallas_call(
        _sb_kernel,
        grid=(batch, n_heads, nq),
        in_specs=[pl.BlockSpec((blk, HEAD), lambda b, h, i: (b * nq + i, c0 + h)),
                  pl.BlockSpec((seq, HEAD), lambda b, h, i: (b, c0 + n_heads + h)),
                  pl.BlockSpec((seq, HEAD), lambda b, h, i: (b, c0 + 2 * n_heads + h)),
                  pl.BlockSpec((blk, HEAD), lambda b, h, i: (b * nq + i, c0 + 3 * n_heads + h)),
                  pl.BlockSpec((2 * blk, blk), lambda b, h, i: (0, 0))],
        out_specs=pl.BlockSpec((blk, HEAD), lambda b, h, i: (b * nq + i, h)),
        out_shape=jax.ShapeDtypeStruct((m, n_heads * HEAD), BF16),
        scratch_shapes=[pltpu.VMEM((blk, HEAD), F32), pltpu.VMEM((blk, HEAD), F32)],
        compiler_params=pltpu.CompilerParams(
            dimension_semantics=("parallel", "parallel", "arbitrary"),
            vmem_limit_bytes=VMEM_LIMIT),
        name="stick_breaking",
    )(pbf, pbf, pbf, pbf, later2)


def _tail_kernel(alpha, ya_ref, yb_ref, gt_ref, x_ref, p_ref, lig_ref, lib_ref,
                 wa_ref, wb_ref, wo_ref, wpg_ref, bpg_ref, wpe_ref, lg_ref, lb_ref, o_ref):
    d = o_ref.shape[1]
    ma = _dot(ya_ref[...], wa_ref[...])
    mb = _dot(yb_ref[...], wb_ref[...])
    mm = gt_ref[:, :d].astype(F32) * ma + gt_ref[:, d:].astype(F32) * mb
    h = _layer_norm(x_ref[...], lig_ref[...], lib_ref[...])
    r = alpha * h + _dot(mm.astype(BF16), wo_ref[...])
    pg = _sigmoid(_dot(r.astype(BF16), wpg_ref[...]) + bpg_ref[...])
    pe = _dot(p_ref[...].astype(BF16), wpe_ref[...])
    r = r + pg * pe
    o_ref[...] = _layer_norm(r, lg_ref[...], lb_ref[...])


def _tail_call(alpha, ya, yb, gates, x, p, lig, lib, wa, wb, wo, wpg, bpg, wpe, lg, lb, tm=256):
    m, d = x.shape
    tm = min(tm, m)

    def act(a):
        return pl.BlockSpec((tm, a.shape[1]), lambda i: (i, 0))

    def const(a):
        return pl.BlockSpec(a.shape, lambda i: (0, 0), pipeline_mode=pl.Buffered(1))

    return pl.pallas_call(
        functools.partial(_tail_kernel, alpha),
        grid=(m // tm,),
        in_specs=[act(ya), act(yb), act(gates), act(x), act(p), const(lig), const(lib),
                  const(wa), const(wb), const(wo), const(wpg), const(bpg), const(wpe),
                  const(lg), const(lb)],
        out_specs=pl.BlockSpec((tm, d), lambda i: (i, 0)),
        out_shape=jax.ShapeDtypeStruct((m, d), F32),
        compiler_params=pltpu.CompilerParams(
            dimension_semantics=("parallel",), vmem_limit_bytes=VMEM_LIMIT),
        name="tail",
    )(ya, yb, gates, x, p, lig, lib, wa, wb, wo, wpg, bpg, wpe, lg, lb)


def kernel(x, p, ln_in_g, ln_in_b, w_in, hg_lb_logits, hg_norm_g, w_merge, b_merge,
           w_br_hg, w_br_sb, w_out, w_pe, w_pg, b_pg, ln_g, ln_b):
    batch, seq, d = x.shape
    depth = w_in.shape[0]
    assert depth == 1, "kernel is written for a single layer"
    hg_w = w_br_hg.shape[1]
    sb_w = w_br_sb.shape[1]
    assert w_in.shape[2] == 4 * hg_w + 4 * sb_w
    hg_heads, sb_heads = hg_w // HEAD, sb_w // HEAD
    alpha = float((2 * depth) ** 0.25)
    m = batch * seq

    x2 = x.reshape(m, d)
    p2 = p[0].reshape(m, p.shape[-1])
    row = lambda a: a.reshape(1, -1)

    w = w_in[0]
    scale = HEAD ** -0.5
    w_hf = w[:, hg_w:2 * hg_w].astype(BF16)
    sq0 = 4 * hg_w
    w_bf = jnp.concatenate(
        [w[:, :hg_w], w[:, 2 * hg_w:sq0], w[:, sq0:sq0 + sb_w] * scale, w[:, sq0 + sb_w:]],
        axis=1).astype(BF16)

    h = _ln_call(x2, row(ln_in_g), row(ln_in_b))
    pbf = _matmul_call(h, w_bf, None, BF16, "proj_bf16")
    pf = _matmul_call(h, w_hf, None, F32, "proj_f32")
    gates = _matmul_call(h, w_merge[0].astype(BF16), row(b_merge[0]), BF16, "merge_gates")

    ya = _hgrn2_call(pbf, pf, hg_lb_logits, row(hg_norm_g[0]), batch, seq, hg_heads)

    idx = jnp.arange(SB_BLOCK)
    later = (idx[:, None] > idx[None, :]).astype(BF16)
    later2 = jnp.concatenate([later, later], axis=0)
    yb = _sb_call(pbf, later2, batch, seq, sb_heads, 3 * hg_w)

    out = _tail_call(alpha, ya, yb, gates, x2, p2, row(ln_in_g), row(ln_in_b),
                     w_br_hg[0].astype(BF16), w_br_sb[0].astype(BF16), w_out[0].astype(BF16),
                     w_pg[0].astype(BF16), row(b_pg[0]), w_pe[0].astype(BF16),
                     row(ln_g[0]), row(ln_b[0]))
    return out.reshape(batch, seq, d)
```

```python
import functools

import jax
import jax.numpy as jnp
from jax import lax
from jax.experimental import pallas as pl
from jax.experimental.pallas import tpu as pltpu

LN_EPS = 1e-5
HEAD = 128
HG_CHUNK = 128
HG_HEADS_PER_STEP = 2
HG_ROWS_PER_STEP = 512
SB_BLOCK = 256
SB_LOG_ZERO = -104.0
VMEM_LIMIT = 52 * 1024 * 1024

F32 = jnp.float32
BF16 = jnp.bfloat16


def _dot(a, b):
    return jnp.dot(a, b, preferred_element_type=F32)


def _dot_nt(a, b):
    return lax.dot_general(a, b, (((1,), (1,)), ((), ())), preferred_element_type=F32)


def _layer_norm(x, g, b):
    mu = jnp.mean(x, axis=-1, keepdims=True)
    xc = x - mu
    var = jnp.mean(xc * xc, axis=-1, keepdims=True)
    return xc * lax.rsqrt(var + LN_EPS) * g + b


def _sigmoid(x):
    return 1.0 / (1.0 + jnp.exp(-x))


def _ln_kernel(x_ref, g_ref, b_ref, o_ref):
    o_ref[...] = _layer_norm(x_ref[...], g_ref[...], b_ref[...]).astype(o_ref.dtype)


def _ln_call(x, g, b, tm=512):
    m, d = x.shape
    return pl.pallas_call(
        _ln_kernel,
        grid=(m // tm,),
        in_specs=[pl.BlockSpec((tm, d), lambda i: (i, 0)),
                  pl.BlockSpec((1, d), lambda i: (0, 0)),
                  pl.BlockSpec((1, d), lambda i: (0, 0))],
        out_specs=pl.BlockSpec((tm, d), lambda i: (i, 0)),
        out_shape=jax.ShapeDtypeStruct((m, d), BF16),
        compiler_params=pltpu.CompilerParams(
            dimension_semantics=("parallel",), vmem_limit_bytes=VMEM_LIMIT),
        name="ln_in",
    )(x, g, b)


def _matmul_kernel(h_ref, w_ref, o_ref):
    o_ref[...] = _dot(h_ref[...], w_ref[...]).astype(o_ref.dtype)


def _matmul_gate_kernel(h_ref, w_ref, b_ref, o_ref):
    o_ref[...] = _sigmoid(_dot(h_ref[...], w_ref[...]) + b_ref[...]).astype(o_ref.dtype)


def _matmul_call(h, w, bias, out_dtype, name, tm=1024, tn=1024):
    m, k = h.shape
    n = w.shape[1]
    tm = min(tm, m)
    in_specs = [pl.BlockSpec((tm, k), lambda i, j: (i, 0)),
                pl.BlockSpec((k, tn), lambda i, j: (0, j))]
    args = [h, w]
    kern = _matmul_kernel
    if bias is not None:
        in_specs.append(pl.BlockSpec((1, tn), lambda i, j: (0, j)))
        args.append(bias)
        kern = _matmul_gate_kernel
    return pl.pallas_call(
        kern,
        grid=(m // tm, n // tn),
        in_specs=in_specs,
        out_specs=pl.BlockSpec((tm, tn), lambda i, j: (i, j)),
        out_shape=jax.ShapeDtypeStruct((m, n), out_dtype),
        compiler_params=pltpu.CompilerParams(
            dimension_semantics=("parallel", "parallel"), vmem_limit_bytes=VMEM_LIMIT),
        name=name,
    )(*args)


def _hgrn2_kernel(q_ref, f_ref, i_ref, g_ref, lbl_ref, ng_ref, o_ref, st_ref):
    c = HG_CHUNK
    n_chunks = q_ref.shape[0] // c
    n_heads = q_ref.shape[1] // HEAD

    @pl.when(pl.program_id(2) == 0)
    def _():
        st_ref[...] = jnp.zeros_like(st_ref)

    lbl = lbl_ref[...]
    e = jnp.exp(lbl - jnp.max(lbl, axis=0, keepdims=True))
    lb_all = e[0:1, :] / jnp.sum(e, axis=0, keepdims=True)

    row = lax.broadcasted_iota(jnp.int32, (c, c), 0)
    col = lax.broadcasted_iota(jnp.int32, (c, c), 1)
    causal = col <= row
    tril = jnp.where(causal, 1.0, 0.0).astype(BF16)
    tril2 = jnp.concatenate([tril, tril], axis=1)

    for hh in range(n_heads):
        cs = slice(hh * HEAD, (hh + 1) * HEAD)
        lb = lb_all[:, cs]
        ng = ng_ref[:, cs]
        for ci in range(n_chunks):
            rs = slice(ci * c, (ci + 1) * c)
            f = lb + (1.0 - lb) * _sigmoid(f_ref[rs, cs].astype(F32))
            logf = jnp.log(f)
            kk = 1.0 - f
            hi = logf.astype(BF16)
            lo = (logf - hi.astype(F32)).astype(BF16)
            bc = _dot(tril2, jnp.concatenate([hi, lo], axis=0))
            b_last = bc[c - 1:c, :]
            b_mid = bc[c // 2 - 1:c // 2, :]
            q = q_ref[rs, cs].astype(F32)
            q_dec = (q * jnp.exp(bc)).astype(BF16)
            q_n = (q * jnp.exp(bc - b_mid)).astype(BF16)
            k_n = (kk * jnp.exp(b_mid - bc)).astype(BF16)
            k_end = (kk * jnp.exp(b_last - bc)).astype(BF16)
            a = jnp.where(causal, _dot_nt(q_n, k_n), 0.0).astype(BF16)
            v_t = i_ref[rs, cs].astype(F32).T.astype(BF16)
            st = st_ref[hh]
            o = _dot_nt(jnp.concatenate([a, q_dec], axis=1),
                        jnp.concatenate([v_t, st.astype(BF16)], axis=1))
            st_ref[hh] = st * jnp.exp(b_last) + _dot(v_t, k_end)
            ms = jnp.mean(o * o, axis=-1, keepdims=True)
            g = g_ref[rs, cs].astype(F32)
            y = o * lax.rsqrt(ms + LN_EPS) * ng * (g * _sigmoid(g))
            o_ref[rs, cs] = y.astype(o_ref.dtype)


def _hgrn2_call(pbf, pf, lb_logits, norm_g, batch, seq, n_heads):
    m = batch * seq
    hb, tt = HG_HEADS_PER_STEP, min(HG_ROWS_PER_STEP, seq)
    w = hb * HEAD
    nt = seq // tt
    n_hg = n_heads // hb
    sec = n_heads * HEAD // w

    def rows(b, h, t):
        return b * nt + t

    return pl.pallas_call(
        _hgrn2_kernel,
        grid=(batch, n_hg, nt),
        in_specs=[pl.BlockSpec((tt, w), lambda b, h, t: (rows(b, h, t), h)),
                  pl.BlockSpec((tt, w), lambda b, h, t: (rows(b, h, t), h)),
                  pl.BlockSpec((tt, w), lambda b, h, t: (rows(b, h, t), sec + h)),
                  pl.BlockSpec((tt, w), lambda b, h, t: (rows(b, h, t), 2 * sec + h)),
                  pl.BlockSpec((2, w), lambda b, h, t: (0, h)),
                  pl.BlockSpec((1, w), lambda b, h, t: (0, h))],
        out_specs=pl.BlockSpec((tt, w), lambda b, h, t: (rows(b, h, t), h)),
        out_shape=jax.ShapeDtypeStruct((m, n_heads * HEAD), BF16),
        scratch_shapes=[pltpu.VMEM((hb, HEAD, HEAD), F32)],
        compiler_params=pltpu.CompilerParams(
            dimension_semantics=("parallel", "parallel", "arbitrary"),
            vmem_limit_bytes=VMEM_LIMIT),
        name="hgrn2",
    )(pbf, pf, pbf, pbf, lb_logits, norm_g)


def _softplus(z):
    return jnp.maximum(z, 0.0) + jnp.log(1.0 + jnp.exp(-jnp.abs(z)))


def _sb_kernel(q_ref, k_ref, v_ref, g_ref, later_ref, o_ref, acc_ref, r_ref):
    blk = SB_BLOCK
    qi = pl.program_id(2)
    q = q_ref[...]
    later2 = later_ref[...]

    def block(kb, diagonal):
        start = pl.multiple_of(kb * blk, blk)
        k = k_ref[pl.ds(start, blk), :]
        v = v_ref[pl.ds(start, blk), :]
        z = _dot_nt(q, k)
        sp = _softplus(z)
        l = -sp
        if diagonal:
            qrow = lax.broadcasted_iota(jnp.int32, (blk, blk), 0)
            kcol = lax.broadcasted_iota(jnp.int32, (blk, blk), 1)
            strict = kcol < qrow
            l = jnp.where(strict, l, 0.0)
        hi = l.astype(BF16)
        lo = (l - hi.astype(F32)).astype(BF16)
        suffix = _dot(jnp.concatenate([hi, lo], axis=1), later2)
        r = r_ref[...]
        log_a = (z - sp) + suffix + jnp.concatenate([r] * (blk // HEAD), axis=1)
        a = jnp.exp(log_a)
        if diagonal:
            a = jnp.where(strict, a, 0.0)
        acc_ref[...] += _dot(a.astype(BF16), v)
        r_ref[...] = r + jnp.sum(l, axis=-1, keepdims=True)

    acc_ref[...] = jnp.zeros_like(acc_ref)
    r_ref[...] = jnp.zeros_like(r_ref)
    block(qi, True)

    def cond(carry):
        kb, r_max = carry
        return jnp.logical_and(kb >= 0, r_max > SB_LOG_ZERO)

    def body(carry):
        kb, _ = carry
        block(kb, False)
        return kb - 1, jnp.max(r_ref[...])

    lax.while_loop(cond, body, (qi - 1, jnp.max(r_ref[...])))

    g = g_ref[...].astype(F32)
    o_ref[...] = (acc_ref[...] * (g * _sigmoid(g))).astype(o_ref.dtype)


def _sb_call(pbf, later2, batch, seq, n_heads, col0):
    m = batch * seq
    blk = SB_BLOCK
    nq = seq // blk
    c0 = col0 // HEAD

    return pl.pallas_call(
        _sb_kernel,
        grid=(batch, n_heads, nq),
        in_specs=[pl.BlockSpec((blk, HEAD), lambda b, h, i: (b * nq + i, c0 + h)),
                  pl.BlockSpec((seq, HEAD), lambda b, h, i: (b, c0 + n_heads + h)),
                  pl.BlockSpec((seq, HEAD), lambda b, h, i: (b, c0 + 2 * n_heads + h)),
                  pl.BlockSpec((blk, HEAD), lambda b, h, i: (b * nq + i, c0 + 3 * n_heads + h)),
                  pl.BlockSpec((2 * blk, blk), lambda b, h, i: (0, 0))],
        out_specs=pl.BlockSpec((blk, HEAD), lambda b, h, i: (b * nq + i, h)),
        out_shape=jax.ShapeDtypeStruct((m, n_heads * HEAD), BF16),
        scratch_shapes=[pltpu.VMEM((blk, HEAD), F32), pltpu.VMEM((blk, HEAD), F32)],
        compiler_params=pltpu.CompilerParams(
            dimension_semantics=("parallel", "parallel", "arbitrary"),
            vmem_limit_bytes=VMEM_LIMIT),
        name="stick_breaking",
    )(pbf, pbf, pbf, pbf, later2)


def _tail_kernel(alpha, ya_ref, yb_ref, gt_ref, x_ref, p_ref, lig_ref, lib_ref,
                 wa_ref, wb_ref, wo_ref, wpg_ref, bpg_ref, wpe_ref, lg_ref, lb_ref, o_ref):
    d = o_ref.shape[1]
    ma = _dot(ya_ref[...], wa_ref[...])
    mb = _dot(yb_ref[...], wb_ref[...])
    mm = gt_ref[:, :d].astype(F32) * ma + gt_ref[:, d:].astype(F32) * mb
    h = _layer_norm(x_ref[...], lig_ref[...], lib_ref[...])
    r = alpha * h + _dot(mm.astype(BF16), wo_ref[...])
    pg = _sigmoid(_dot(r.astype(BF16), wpg_ref[...]) + bpg_ref[...])
    pe = _dot(p_ref[...].astype(BF16), wpe_ref[...])
    r = r + pg * pe
    o_ref[...] = _layer_norm(r, lg_ref[...], lb_ref[...])


def _tail_call(alpha, ya, yb, gates, x, p, lig, lib, wa, wb, wo, wpg, bpg, wpe, lg, lb, tm=256):
    m, d = x.shape
    tm = min(tm, m)

    def act(a):
        return pl.BlockSpec((tm, a.shape[1]), lambda i: (i, 0))

    def const(a):
        return pl.BlockSpec(a.shape, lambda i: (0, 0), pipeline_mode=pl.Buffered(1))

    return pl.pallas_call(
        functools.partial(_tail_kernel, alpha),
        grid=(m // tm,),
        in_specs=[act(ya), act(yb), act(gates), act(x), act(p), const(lig), const(lib),
                  const(wa), const(wb), const(wo), const(wpg), const(bpg), const(wpe),
                  const(lg), const(lb)],
        out_specs=pl.BlockSpec((tm, d), lambda i: (i, 0)),
        out_shape=jax.ShapeDtypeStruct((m, d), F32),
        compiler_params=pltpu.CompilerParams(
            dimension_semantics=("parallel",), vmem_limit_bytes=VMEM_LIMIT),
        name="tail",
    )(ya, yb, gates, x, p, lig, lib, wa, wb, wo, wpg, bpg, wpe, lg, lb)


def kernel(x, p, ln_in_g, ln_in_b, w_in, hg_lb_logits, hg_norm_g, w_merge, b_merge,
           w_br_hg, w_br_sb, w_out, w_pe, w_pg, b_pg, ln_g, ln_b):
    batch, seq, d = x.shape
    depth = w_in.shape[0]
    assert depth == 1, "kernel is written for a single layer"
    hg_w = w_br_hg.shape[1]
    sb_w = w_br_sb.shape[1]
    assert w_in.shape[2] == 4 * hg_w + 4 * sb_w
    hg_heads, sb_heads = hg_w // HEAD, sb_w // HEAD
    alpha = float((2 * depth) ** 0.25)
    m = batch * seq

    x2 = x.reshape(m, d)
    p2 = p[0].reshape(m, p.shape[-1])
    row = lambda a: a.reshape(1, -1)

    w = w_in[0]
    scale = HEAD ** -0.5
    w_hf = w[:, hg_w:2 * hg_w].astype(BF16)
    sq0 = 4 * hg_w
    w_bf = jnp.concatenate(
        [w[:, :hg_w], w[:, 2 * hg_w:sq0], w[:, sq0:sq0 + sb_w] * scale, w[:, sq0 + sb_w:]],
        axis=1).astype(BF16)

    h = _ln_call(x2, row(ln_in_g), row(ln_in_b))
    pbf = _matmul_call(h, w_bf, None, BF16, "proj_bf16")
    pf = _matmul_call(h, w_hf, None, F32, "proj_f32")
    gates = _matmul_call(h, w_merge[0].astype(BF16), row(b_merge[0]), BF16, "merge_gates")

    ya = _hgrn2_call(pbf, pf, hg_lb_logits, row(hg_norm_g[0]), batch, seq, hg_heads)

    idx = jnp.arange(SB_BLOCK)
    later = (idx[:, None] > idx[None, :]).astype(BF16)
    later2 = jnp.concatenate([later, later], axis=0)
    yb = _sb_call(pbf, later2, batch, seq, sb_heads, 3 * hg_w)

    out = _tail_call(alpha, ya, yb, gates, x2, p2, row(ln_in_g), row(ln_in_b),
                     w_br_hg[0].astype(BF16), w_br_sb[0].astype(BF16), w_out[0].astype(BF16),
                     w_pg[0].astype(BF16), row(b_pg[0]), w_pe[0].astype(BF16),
                     row(ln_g[0]), row(ln_b[0]))
    return out.reshape(batch, seq, d)
```

```python
import functools

import jax
import jax.numpy as jnp
from jax import lax
from jax.experimental import pallas as pl
from jax.experimental.pallas import tpu as pltpu

LN_EPS = 1e-5
HEAD = 128
HG_CHUNK = 128
HG_HEADS_PER_STEP = 2
HG_ROWS_PER_STEP = 512
SB_BLOCK = 256
SB_LOG_ZERO = -104.0
VMEM_LIMIT = 52 * 1024 * 1024

F32 = jnp.float32
BF16 = jnp.bfloat16


def _dot(a, b):
    return jnp.dot(a, b, preferred_element_type=F32)


def _dot_nt(a, b):
    return lax.dot_general(a, b, (((1,), (1,)), ((), ())), preferred_element_type=F32)


def _layer_norm(x, g, b):
    mu = jnp.mean(x, axis=-1, keepdims=True)
    xc = x - mu
    var = jnp.mean(xc * xc, axis=-1, keepdims=True)
    return xc * lax.rsqrt(var + LN_EPS) * g + b


def _sigmoid(x):
    return 1.0 / (1.0 + jnp.exp(-x))


def _ln_kernel(x_ref, g_ref, b_ref, o_ref):
    o_ref[...] = _layer_norm(x_ref[...], g_ref[...], b_ref[...]).astype(o_ref.dtype)


def _ln_call(x, g, b, tm=512):
    m, d = x.shape
    return pl.pallas_call(
        _ln_kernel,
        grid=(m // tm,),
        in_specs=[pl.BlockSpec((tm, d), lambda i: (i, 0)),
                  pl.BlockSpec((1, d), lambda i: (0, 0)),
                  pl.BlockSpec((1, d), lambda i: (0, 0))],
        out_specs=pl.BlockSpec((tm, d), lambda i: (i, 0)),
        out_shape=jax.ShapeDtypeStruct((m, d), BF16),
        compiler_params=pltpu.CompilerParams(
            dimension_semantics=("parallel",), vmem_limit_bytes=VMEM_LIMIT),
        name="ln_in",
    )(x, g, b)


def _matmul_kernel(h_ref, w_ref, o_ref):
    o_ref[...] = _dot(h_ref[...], w_ref[...]).astype(o_ref.dtype)


def _matmul_gate_kernel(h_ref, w_ref, b_ref, o_ref):
    o_ref[...] = _sigmoid(_dot(h_ref[...], w_ref[...]) + b_ref[...]).astype(o_ref.dtype)


def _matmul_call(h, w, bias, out_dtype, name, tm=1024, tn=1024):
    m, k = h.shape
    n = w.shape[1]
    tm = min(tm, m)
    in_specs = [pl.BlockSpec((tm, k), lambda i, j: (i, 0)),
                pl.BlockSpec((k, tn), lambda i, j: (0, j))]
    args = [h, w]
    kern = _matmul_kernel
    if bias is not None:
        in_specs.append(pl.BlockSpec((1, tn), lambda i, j: (0, j)))
        args.append(bias)
        kern = _matmul_gate_kernel
    return pl.pallas_call(
        kern,
        grid=(m // tm, n // tn),
        in_specs=in_specs,
        out_specs=pl.BlockSpec((tm, tn), lambda i, j: (i, j)),
        out_shape=jax.ShapeDtypeStruct((m, n), out_dtype),
        compiler_params=pltpu.CompilerParams(
            dimension_semantics=("parallel", "parallel"), vmem_limit_bytes=VMEM_LIMIT),
        name=name,
    )(*args)


def _hgrn2_kernel(q_ref, f_ref, i_ref, g_ref, lbl_ref, ng_ref, o_ref, st_ref):
    c = HG_CHUNK
    n_chunks = q_ref.shape[0] // c
    n_heads = q_ref.shape[1] // HEAD

    @pl.when(pl.program_id(2) == 0)
    def _():
        st_ref[...] = jnp.zeros_like(st_ref)

    lbl = lbl_ref[...]
    e = jnp.exp(lbl - jnp.max(lbl, axis=0, keepdims=True))
    lb_all = e[0:1, :] / jnp.sum(e, axis=0, keepdims=True)

    row = lax.broadcasted_iota(jnp.int32, (c, c), 0)
    col = lax.broadcasted_iota(jnp.int32, (c, c), 1)
    causal = col <= row
    tril = jnp.where(causal, 1.0, 0.0).astype(BF16)
    tril2 = jnp.concatenate([tril, tril], axis=1)

    for hh in range(n_heads):
        cs = slice(hh * HEAD, (hh + 1) * HEAD)
        lb = lb_all[:, cs]
        ng = ng_ref[:, cs]
        for ci in range(n_chunks):
            rs = slice(ci * c, (ci + 1) * c)
            f = lb + (1.0 - lb) * _sigmoid(f_ref[rs, cs].astype(F32))
            logf = jnp.log(f)
            kk = 1.0 - f
            hi = logf.astype(BF16)
            lo = (logf - hi.astype(F32)).astype(BF16)
            bc = _dot(tril2, jnp.concatenate([hi, lo], axis=0))
            b_last = bc[c - 1:c, :]
            b_mid = bc[c // 2 - 1:c // 2, :]
            q = q_ref[rs, cs].astype(F32)
            q_dec = (q * jnp.exp(bc)).astype(BF16)
            q_n = (q * jnp.exp(bc - b_mid)).astype(BF16)
            k_n = (kk * jnp.exp(b_mid - bc)).astype(BF16)
            k_end = (kk * jnp.exp(b_last - bc)).astype(BF16)
            a = jnp.where(causal, _dot_nt(q_n, k_n), 0.0).astype(BF16)
            v_t = i_ref[rs, cs].astype(F32).T.astype(BF16)
            st = st_ref[hh]
            o = _dot_nt(jnp.concatenate([a, q_dec], axis=1),
                        jnp.concatenate([v_t, st.astype(BF16)], axis=1))
            st_ref[hh] = st * jnp.exp(b_last) + _dot(v_t, k_end)
            ms = jnp.mean(o * o, axis=-1, keepdims=True)
            g = g_ref[rs, cs].astype(F32)
            y = o * lax.rsqrt(ms + LN_EPS) * ng * (g * _sigmoid(g))
            o_ref[rs, cs] = y.astype(o_ref.dtype)


def _hgrn2_call(pbf, pf, lb_logits, norm_g, batch, seq, n_heads):
    m = batch * seq
    hb, tt = HG_HEADS_PER_STEP, min(HG_ROWS_PER_STEP, seq)
    w = hb * HEAD
    nt = seq // tt
    n_hg = n_heads // hb
    sec = n_heads * HEAD // w

    def rows(b, h, t):
        return b * nt + t

    return pl.pallas_call(
        _hgrn2_kernel,
        grid=(batch, n_hg, nt),
        in_specs=[pl.BlockSpec((tt, w), lambda b, h, t: (rows(b, h, t), h)),
                  pl.BlockSpec((tt, w), lambda b, h, t: (rows(b, h, t), h)),
                  pl.BlockSpec((tt, w), lambda b, h, t: (rows(b, h, t), sec + h)),
                  pl.BlockSpec((tt, w), lambda b, h, t: (rows(b, h, t), 2 * sec + h)),
                  pl.BlockSpec((2, w), lambda b, h, t: (0, h)),
                  pl.BlockSpec((1, w), lambda b, h, t: (0, h))],
        out_specs=pl.BlockSpec((tt, w), lambda b, h, t: (rows(b, h, t), h)),
        out_shape=jax.ShapeDtypeStruct((m, n_heads * HEAD), BF16),
        scratch_shapes=[pltpu.VMEM((hb, HEAD, HEAD), F32)],
        compiler_params=pltpu.CompilerParams(
            dimension_semantics=("parallel", "parallel", "arbitrary"),
            vmem_limit_bytes=VMEM_LIMIT),
        name="hgrn2",
    )(pbf, pf, pbf, pbf, lb_logits, norm_g)


def _softplus(z):
    return jnp.maximum(z, 0.0) + jnp.log(1.0 + jnp.exp(-jnp.abs(z)))


def _sb_block(q, k, v, later2, r, diagonal):
    blk = q.shape[0]
    z = _dot_nt(q, k)
    sp = _softplus(z)
    l = -sp
    if diagonal:
        qrow = lax.broadcasted_iota(jnp.int32, (blk, blk), 0)
        kcol = lax.broadcasted_iota(jnp.int32, (blk, blk), 1)
        strict = kcol < qrow
        l = jnp.where(strict, l, 0.0)
    hi = l.astype(BF16)
    lo = (l - hi.astype(F32)).astype(BF16)
    suffix = _dot(jnp.concatenate([hi, lo], axis=1), later2)
    log_a = (z - sp) + suffix
    if r is not None:
        log_a = log_a + jnp.concatenate([r] * (blk // HEAD), axis=1)
    a = jnp.exp(log_a)
    if diagonal:
        a = jnp.where(strict, a, 0.0)
    return _dot(a.astype(BF16), v), jnp.sum(l, axis=-1, keepdims=True)


def _sb_kernel(k_col0, seq, q_ref, kd_ref, vd_ref, kp_ref, vp_ref, g_ref, later_ref, kv_hbm,
               o_ref, acc_ref, r_ref, kbuf, vbuf, sem):
    blk = SB_BLOCK
    b = pl.program_id(0)
    qi = pl.program_id(1)
    width = q_ref.shape[1]
    n_heads = width // HEAD
    later2 = later_ref[...]
    has_prev = qi > 0

    r_max = None
    for h in range(n_heads):
        cs = slice(h * HEAD, (h + 1) * HEAD)
        q = q_ref[:, cs]
        acc, rs = _sb_block(q, kd_ref[:, cs], vd_ref[:, cs], later2, None, True)
        r = jnp.broadcast_to(rs, (blk, HEAD))
        vp = jnp.where(has_prev, vp_ref[:, cs], jnp.zeros((blk, HEAD), BF16))
        acc2, rs2 = _sb_block(q, kp_ref[:, cs], vp, later2, r, False)
        r = r + rs2
        acc_ref[h] = acc + acc2
        r_ref[h] = r
        r_max = r if r_max is None else jnp.maximum(r_max, r)

    def cond(carry):
        kb, r_top = carry
        return jnp.logical_and(kb >= 0, r_top > SB_LOG_ZERO)

    def body(carry):
        kb, _ = carry
        row0 = pl.multiple_of(b * seq + kb * blk, blk)
        ck = pltpu.make_async_copy(
            kv_hbm.at[pl.ds(row0, blk), pl.ds(k_col0, width)], kbuf, sem.at[0])
        cv = pltpu.make_async_copy(
            kv_hbm.at[pl.ds(row0, blk), pl.ds(k_col0 + width, width)], vbuf, sem.at[1])
        ck.start()
        cv.start()
        ck.wait()
        cv.wait()
        r_top = None
        for h in range(n_heads):
            cs = slice(h * HEAD, (h + 1) * HEAD)
            r = r_ref[h]
            acc2, rs2 = _sb_block(q_ref[:, cs], kbuf[:, cs], vbuf[:, cs], later2, r, False)
            r = r + rs2
            acc_ref[h] += acc2
            r_ref[h] = r
            r_top = r if r_top is None else jnp.maximum(r_top, r)
        return kb - 1, jnp.max(r_top)

    lax.while_loop(cond, body, (qi - 2, jnp.max(r_max)))

    for h in range(n_heads):
        cs = slice(h * HEAD, (h + 1) * HEAD)
        g = g_ref[:, cs].astype(F32)
        o_ref[:, cs] = (acc_ref[h] * (g * _sigmoid(g))).astype(o_ref.dtype)


def _sb_call(pbf, later2, batch, seq, n_heads, col0):
    m = batch * seq
    blk = SB_BLOCK
    nq = seq // blk
    width = n_heads * HEAD
    c0 = col0 // width

    def tile(sec):
        return pl.BlockSpec((blk, width), lambda b, i: (b * nq + i, c0 + sec))

    def prev(sec):
        return pl.BlockSpec((blk, width), lambda b, i: (b * nq + jnp.maximum(i - 1, 0), c0 + sec))

    return pl.pallas_call(
        functools.partial(_sb_kernel, col0 + width, seq),
        grid=(batch, nq),
        in_specs=[tile(0), tile(1), tile(2), prev(1), prev(2), tile(3),
                  pl.BlockSpec((2 * blk, blk), lambda b, i: (0, 0)),
                  pl.BlockSpec(memory_space=pl.ANY)],
        out_specs=pl.BlockSpec((blk, width), lambda b, i: (b * nq + i, 0)),
        out_shape=jax.ShapeDtypeStruct((m, width), BF16),
        scratch_shapes=[pltpu.VMEM((n_heads, blk, HEAD), F32),
                        pltpu.VMEM((n_heads, blk, HEAD), F32),
                        pltpu.VMEM((blk, width), BF16),
                        pltpu.VMEM((blk, width), BF16),
                        pltpu.SemaphoreType.DMA((2,))],
        compiler_params=pltpu.CompilerParams(
            dimension_semantics=("parallel", "arbitrary"),
            vmem_limit_bytes=VMEM_LIMIT),
        name="stick_breaking",
    )(pbf, pbf, pbf, pbf, pbf, pbf, later2, pbf)


def _tail_kernel(alpha, ya_ref, yb_ref, gt_ref, x_ref, p_ref, lig_ref, lib_ref,
                 wa_ref, wb_ref, wo_ref, wpg_ref, bpg_ref, wpe_ref, lg_ref, lb_ref, o_ref):
    d = o_ref.shape[1]
    ma = _dot(ya_ref[...], wa_ref[...])
    mb = _dot(yb_ref[...], wb_ref[...])
    mm = gt_ref[:, :d].astype(F32) * ma + gt_ref[:, d:].astype(F32) * mb
    h = _layer_norm(x_ref[...], lig_ref[...], lib_ref[...])
    r = alpha * h + _dot(mm.astype(BF16), wo_ref[...])
    pg = _sigmoid(_dot(r.astype(BF16), wpg_ref[...]) + bpg_ref[...])
    pe = _dot(p_ref[...].astype(BF16), wpe_ref[...])
    r = r + pg * pe
    o_ref[...] = _layer_norm(r, lg_ref[...], lb_ref[...])


def _tail_call(alpha, ya, yb, gates, x, p, lig, lib, wa, wb, wo, wpg, bpg, wpe, lg, lb, tm=256):
    m, d = x.shape
    tm = min(tm, m)

    def act(a):
        return pl.BlockSpec((tm, a.shape[1]), lambda i: (i, 0))

    def const(a):
        return pl.BlockSpec(a.shape, lambda i: (0, 0), pipeline_mode=pl.Buffered(1))

    return pl.pallas_call(
        functools.partial(_tail_kernel, alpha),
        grid=(m // tm,),
        in_specs=[act(ya), act(yb), act(gates), act(x), act(p), const(lig), const(lib),
                  const(wa), const(wb), const(wo), const(wpg), const(bpg), const(wpe),
                  const(lg), const(lb)],
        out_specs=pl.BlockSpec((tm, d), lambda i: (i, 0)),
        out_shape=jax.ShapeDtypeStruct((m, d), F32),
        compiler_params=pltpu.CompilerParams(
            dimension_semantics=("parallel",), vmem_limit_bytes=VMEM_LIMIT),
        name="tail",
    )(ya, yb, gates, x, p, lig, lib, wa, wb, wo, wpg, bpg, wpe, lg, lb)


def kernel(x, p, ln_in_g, ln_in_b, w_in, hg_lb_logits, hg_norm_g, w_merge, b_merge,
           w_br_hg, w_br_sb, w_out, w_pe, w_pg, b_pg, ln_g, ln_b):
    batch, seq, d = x.shape
    depth = w_in.shape[0]
    assert depth == 1, "kernel is written for a single layer"
    hg_w = w_br_hg.shape[1]
    sb_w = w_br_sb.shape[1]
    assert w_in.shape[2] == 4 * hg_w + 4 * sb_w
    hg_heads, sb_heads = hg_w // HEAD, sb_w // HEAD
    alpha = float((2 * depth) ** 0.25)
    m = batch * seq

    x2 = x.reshape(m, d)
    p2 = p[0].reshape(m, p.shape[-1])
    row = lambda a: a.reshape(1, -1)

    w = w_in[0]
    scale = HEAD ** -0.5
    w_hf = w[:, hg_w:2 * hg_w].astype(BF16)
    sq0 = 4 * hg_w
    w_bf = jnp.concatenate(
        [w[:, :hg_w], w[:, 2 * hg_w:sq0], w[:, sq0:sq0 + sb_w] * scale, w[:, sq0 + sb_w:]],
        axis=1).astype(BF16)

    h = _ln_call(x2, row(ln_in_g), row(ln_in_b))
    pbf = _matmul_call(h, w_bf, None, BF16, "proj_bf16")
    pf = _matmul_call(h, w_hf, None, F32, "proj_f32")
    gates = _matmul_call(h, w_merge[0].astype(BF16), row(b_merge[0]), BF16, "merge_gates")

    ya = _hgrn2_call(pbf, pf, hg_lb_logits, row(hg_norm_g[0]), batch, seq, hg_heads)

    idx = jnp.arange(SB_BLOCK)
    later = (idx[:, None] > idx[None, :]).astype(BF16)
    later2 = jnp.concatenate([later, later], axis=0)
    yb = _sb_call(pbf, later2, batch, seq, sb_heads, 3 * hg_w)

    out = _tail_call(alpha, ya, yb, gates, x2, p2, row(ln_in_g), row(ln_in_b),
                     w_br_hg[0].astype(BF16), w_br_sb[0].astype(BF16), w_out[0].astype(BF16),
                     w_pg[0].astype(BF16), row(b_pg[0]), w_pe[0].astype(BF16),
                     row(ln_g[0]), row(ln_b[0]))
    return out.reshape(batch, seq, d)
```

```python
import functools

import jax
import jax.numpy as jnp
from jax import lax
from jax.experimental import pallas as pl
from jax.experimental.pallas import tpu as pltpu

LN_EPS = 1e-5
HEAD = 128
HG_CHUNK = 128
HG_HEADS_PER_STEP = 2
HG_ROWS_PER_STEP = 512
SB_BLOCK = 256
SB_LOG_ZERO = -104.0
VMEM_LIMIT = 52 * 1024 * 1024

F32 = jnp.float32
BF16 = jnp.bfloat16


def _dot(a, b):
    return jnp.dot(a, b, preferred_element_type=F32)


def _dot_nt(a, b):
    return lax.dot_general(a, b, (((1,), (1,)), ((), ())), preferred_element_type=F32)


def _layer_norm(x, g, b):
    mu = jnp.mean(x, axis=-1, keepdims=True)
    xc = x - mu
    var = jnp.mean(xc * xc, axis=-1, keepdims=True)
    return xc * lax.rsqrt(var + LN_EPS) * g + b


def _sigmoid(x):
    return 1.0 / (1.0 + jnp.exp(-x))


def _sigmoid_tanh(x):
    return 0.5 * jnp.tanh(0.5 * x) + 0.5


def _ln_kernel(x_ref, g_ref, b_ref, o_ref):
    o_ref[...] = _layer_norm(x_ref[...], g_ref[...], b_ref[...]).astype(o_ref.dtype)


def _ln_call(x, g, b, tm=512):
    m, d = x.shape
    return pl.pallas_call(
        _ln_kernel,
        grid=(m // tm,),
        in_specs=[pl.BlockSpec((tm, d), lambda i: (i, 0)),
                  pl.BlockSpec((1, d), lambda i: (0, 0)),
                  pl.BlockSpec((1, d), lambda i: (0, 0))],
        out_specs=pl.BlockSpec((tm, d), lambda i: (i, 0)),
        out_shape=jax.ShapeDtypeStruct((m, d), BF16),
        compiler_params=pltpu.CompilerParams(
            dimension_semantics=("parallel",), vmem_limit_bytes=VMEM_LIMIT),
        name="ln_in",
    )(x, g, b)


def _matmul_kernel(h_ref, w_ref, o_ref):
    o_ref[...] = _dot(h_ref[...], w_ref[...]).astype(o_ref.dtype)


def _matmul_gate_kernel(h_ref, w_ref, b_ref, o_ref):
    o_ref[...] = _sigmoid(_dot(h_ref[...], w_ref[...]) + b_ref[...]).astype(o_ref.dtype)


def _matmul_call(h, w, bias, out_dtype, name, tm=1024, tn=1024):
    m, k = h.shape
    n = w.shape[1]
    tm = min(tm, m)
    in_specs = [pl.BlockSpec((tm, k), lambda i, j: (i, 0)),
                pl.BlockSpec((k, tn), lambda i, j: (0, j))]
    args = [h, w]
    kern = _matmul_kernel
    if bias is not None:
        in_specs.append(pl.BlockSpec((1, tn), lambda i, j: (0, j)))
        args.append(bias)
        kern = _matmul_gate_kernel
    return pl.pallas_call(
        kern,
        grid=(m // tm, n // tn),
        in_specs=in_specs,
        out_specs=pl.BlockSpec((tm, tn), lambda i, j: (i, j)),
        out_shape=jax.ShapeDtypeStruct((m, n), out_dtype),
        compiler_params=pltpu.CompilerParams(
            dimension_semantics=("parallel", "parallel"), vmem_limit_bytes=VMEM_LIMIT),
        name=name,
    )(*args)


def _hgrn2_kernel(q_ref, f_ref, i_ref, g_ref, lbl_ref, ng_ref, o_ref, st_ref):
    c = HG_CHUNK
    n_chunks = q_ref.shape[0] // c
    n_heads = q_ref.shape[1] // HEAD

    @pl.when(pl.program_id(2) == 0)
    def _():
        st_ref[...] = jnp.zeros_like(st_ref)

    lbl = lbl_ref[...]
    e = jnp.exp(lbl - jnp.max(lbl, axis=0, keepdims=True))
    lb_all = e[0:1, :] / jnp.sum(e, axis=0, keepdims=True)

    row = lax.broadcasted_iota(jnp.int32, (c, c), 0)
    col = lax.broadcasted_iota(jnp.int32, (c, c), 1)
    causal = col <= row
    tril = jnp.where(causal, 1.0, 0.0).astype(BF16)
    tril2 = jnp.concatenate([tril, tril], axis=1)

    heads = [slice(hh * HEAD, (hh + 1) * HEAD) for hh in range(n_heads)]
    chunks = [slice(ci * c, (ci + 1) * c) for ci in range(n_chunks)]
    one_m_lb = 1.0 - lb_all

    prep = []
    for rs in chunks:
        f = lb_all + one_m_lb * _sigmoid_tanh(f_ref[rs, :])
        lf2 = jnp.log2(f)
        hi = lf2.astype(BF16)
        lo = (lf2 - hi.astype(F32)).astype(BF16)
        bc = _dot(tril2, jnp.concatenate([hi, lo], axis=0))
        prep.append((1.0 - f, bc))

    ops = {}
    for ci, rs in enumerate(chunks):
        kk_all, bc_all = prep[ci]
        for hh, cs in enumerate(heads):
            kk, bc = kk_all[:, cs], bc_all[:, cs]
            b_last = bc[c - 1:c, :]
            b_mid = bc[c // 2 - 1:c // 2, :]
            q = q_ref[rs, cs].astype(F32)
            q_dec = (q * jnp.exp2(bc)).astype(BF16)
            q_n = (q * jnp.exp2(bc - b_mid)).astype(BF16)
            k_mid = kk * jnp.exp2(b_mid - bc)
            k_end = (k_mid * jnp.exp2(b_last - b_mid)).astype(BF16)
            a = jnp.where(causal, _dot_nt(q_n, k_mid.astype(BF16)), 0.0).astype(BF16)
            v_t = i_ref[rs, cs].astype(F32).T.astype(BF16)
            ops[ci, hh] = (a, q_dec, k_end, v_t, jnp.exp2(b_last))

    for hh, cs in enumerate(heads):
        ng = ng_ref[:, cs]
        st = st_ref[hh]
        for ci, rs in enumerate(chunks):
            a, q_dec, k_end, v_t, dec = ops[ci, hh]
            o = _dot_nt(jnp.concatenate([a, q_dec], axis=1),
                        jnp.concatenate([v_t, st.astype(BF16)], axis=1))
            st = st * dec + _dot(v_t, k_end)
            ms = jnp.mean(o * o, axis=-1, keepdims=True)
            g = g_ref[rs, cs].astype(F32)
            y = o * lax.rsqrt(ms + LN_EPS) * ng * (g * _sigmoid_tanh(g))
            o_ref[rs, cs] = y.astype(o_ref.dtype)
        st_ref[hh] = st


def _hgrn2_call(pbf, pf, lb_logits, norm_g, batch, seq, n_heads):
    m = batch * seq
    hb, tt = HG_HEADS_PER_STEP, min(HG_ROWS_PER_STEP, seq)
    w = hb * HEAD
    nt = seq // tt
    n_hg = n_heads // hb
    sec = n_heads * HEAD // w

    def rows(b, h, t):
        return b * nt + t

    return pl.pallas_call(
        _hgrn2_kernel,
        grid=(batch, n_hg, nt),
        in_specs=[pl.BlockSpec((tt, w), lambda b, h, t: (rows(b, h, t), h)),
                  pl.BlockSpec((tt, w), lambda b, h, t: (rows(b, h, t), h)),
                  pl.BlockSpec((tt, w), lambda b, h, t: (rows(b, h, t), sec + h)),
                  pl.BlockSpec((tt, w), lambda b, h, t: (rows(b, h, t), 2 * sec + h)),
                  pl.BlockSpec((2, w), lambda b, h, t: (0, h)),
                  pl.BlockSpec((1, w), lambda b, h, t: (0, h))],
        out_specs=pl.BlockSpec((tt, w), lambda b, h, t: (rows(b, h, t), h)),
        out_shape=jax.ShapeDtypeStruct((m, n_heads * HEAD), BF16),
        scratch_shapes=[pltpu.VMEM((hb, HEAD, HEAD), F32)],
        compiler_params=pltpu.CompilerParams(
            dimension_semantics=("parallel", "parallel", "arbitrary"),
            vmem_limit_bytes=VMEM_LIMIT),
        name="hgrn2",
    )(pbf, pf, pbf, pbf, lb_logits, norm_g)


def _softplus(z):
    return jnp.maximum(z, 0.0) + jnp.log(1.0 + jnp.exp(-jnp.abs(z)))


def _sb_block(q, k, v, later2, r, diagonal):
    blk = q.shape[0]
    z = _dot_nt(q, k)
    sp = _softplus(z)
    l = -sp
    if diagonal:
        qrow = lax.broadcasted_iota(jnp.int32, (blk, blk), 0)
        kcol = lax.broadcasted_iota(jnp.int32, (blk, blk), 1)
        strict = kcol < qrow
        l = jnp.where(strict, l, 0.0)
    hi = l.astype(BF16)
    lo = (l - hi.astype(F32)).astype(BF16)
    suffix = _dot(jnp.concatenate([hi, lo], axis=1), later2)
    log_a = (z - sp) + suffix
    if r is not None:
        log_a = log_a + jnp.concatenate([r] * (blk // HEAD), axis=1)
    a = jnp.exp(log_a)
    if diagonal:
        a = jnp.where(strict, a, 0.0)
    return _dot(a.astype(BF16), v), jnp.sum(l, axis=-1, keepdims=True)


def _sb_kernel(k_col0, seq, q_ref, kd_ref, vd_ref, kp_ref, vp_ref, g_ref, later_ref, kv_hbm,
               o_ref, acc_ref, r_ref, kbuf, vbuf, sem):
    blk = SB_BLOCK
    b = pl.program_id(0)
    qi = pl.program_id(1)
    width = q_ref.shape[1]
    n_heads = width // HEAD
    later2 = later_ref[...]
    has_prev = qi > 0

    r_max = None
    for h in range(n_heads):
        cs = slice(h * HEAD, (h + 1) * HEAD)
        q = q_ref[:, cs]
        acc, rs = _sb_block(q, kd_ref[:, cs], vd_ref[:, cs], later2, None, True)
        r = jnp.broadcast_to(rs, (blk, HEAD))
        vp = jnp.where(has_prev, vp_ref[:, cs], jnp.zeros((blk, HEAD), BF16))
        acc2, rs2 = _sb_block(q, kp_ref[:, cs], vp, later2, r, False)
        r = r + rs2
        acc_ref[h] = acc + acc2
        r_ref[h] = r
        r_max = r if r_max is None else jnp.maximum(r_max, r)

    def cond(carry):
        kb, r_top = carry
        return jnp.logical_and(kb >= 0, r_top > SB_LOG_ZERO)

    def body(carry):
        kb, _ = carry
        row0 = pl.multiple_of(b * seq + kb * blk, blk)
        ck = pltpu.make_async_copy(
            kv_hbm.at[pl.ds(row0, blk), pl.ds(k_col0, width)], kbuf, sem.at[0])
        cv = pltpu.make_async_copy(
            kv_hbm.at[pl.ds(row0, blk), pl.ds(k_col0 + width, width)], vbuf, sem.at[1])
        ck.start()
        cv.start()
        ck.wait()
        cv.wait()
        r_top = None
        for h in range(n_heads):
            cs = slice(h * HEAD, (h + 1) * HEAD)
            r = r_ref[h]
            acc2, rs2 = _sb_block(q_ref[:, cs], kbuf[:, cs], vbuf[:, cs], later2, r, False)
            r = r + rs2
            acc_ref[h] += acc2
            r_ref[h] = r
            r_top = r if r_top is None else jnp.maximum(r_top, r)
        return kb - 1, jnp.max(r_top)

    lax.while_loop(cond, body, (qi - 2, jnp.max(r_max)))

    for h in range(n_heads):
        cs = slice(h * HEAD, (h + 1) * HEAD)
        g = g_ref[:, cs].astype(F32)
        o_ref[:, cs] = (acc_ref[h] * (g * _sigmoid(g))).astype(o_ref.dtype)


def _sb_call(pbf, later2, batch, seq, n_heads, col0):
    m = batch * seq
    blk = SB_BLOCK
    nq = seq // blk
    width = n_heads * HEAD
    c0 = col0 // width

    def tile(sec):
        return pl.BlockSpec((blk, width), lambda b, i: (b * nq + i, c0 + sec))

    def prev(sec):
        return pl.BlockSpec((blk, width), lambda b, i: (b * nq + jnp.maximum(i - 1, 0), c0 + sec))

    return pl.pallas_call(
        functools.partial(_sb_kernel, col0 + width, seq),
        grid=(batch, nq),
        in_specs=[tile(0), tile(1), tile(2), prev(1), prev(2), tile(3),
                  pl.BlockSpec((2 * blk, blk), lambda b, i: (0, 0)),
                  pl.BlockSpec(memory_space=pl.ANY)],
        out_specs=pl.BlockSpec((blk, width), lambda b, i: (b * nq + i, 0)),
        out_shape=jax.ShapeDtypeStruct((m, width), BF16),
        scratch_shapes=[pltpu.VMEM((n_heads, blk, HEAD), F32),
                        pltpu.VMEM((n_heads, blk, HEAD), F32),
                        pltpu.VMEM((blk, width), BF16),
                        pltpu.VMEM((blk, width), BF16),
                        pltpu.SemaphoreType.DMA((2,))],
        compiler_params=pltpu.CompilerParams(
            dimension_semantics=("parallel", "arbitrary"),
            vmem_limit_bytes=VMEM_LIMIT),
        name="stick_breaking",
    )(pbf, pbf, pbf, pbf, pbf, pbf, later2, pbf)


def _tail_kernel(alpha, ya_ref, yb_ref, gt_ref, x_ref, p_ref, lig_ref, lib_ref,
                 wa_ref, wb_ref, wo_ref, wpg_ref, bpg_ref, wpe_ref, lg_ref, lb_ref, o_ref):
    d = o_ref.shape[1]
    ma = _dot(ya_ref[...], wa_ref[...])
    mb = _dot(yb_ref[...], wb_ref[...])
    mm = gt_ref[:, :d].astype(F32) * ma + gt_ref[:, d:].astype(F32) * mb
    h = _layer_norm(x_ref[...], lig_ref[...], lib_ref[...])
    r = alpha * h + _dot(mm.astype(BF16), wo_ref[...])
    pg = _sigmoid(_dot(r.astype(BF16), wpg_ref[...]) + bpg_ref[...])
    pe = _dot(p_ref[...].astype(BF16), wpe_ref[...])
    r = r + pg * pe
    o_ref[...] = _layer_norm(r, lg_ref[...], lb_ref[...])


def _tail_call(alpha, ya, yb, gates, x, p, lig, lib, wa, wb, wo, wpg, bpg, wpe, lg, lb, tm=256):
    m, d = x.shape
    tm = min(tm, m)

    def act(a):
        return pl.BlockSpec((tm, a.shape[1]), lambda i: (i, 0))

    def const(a):
        return pl.BlockSpec(a.shape, lambda i: (0, 0), pipeline_mode=pl.Buffered(1))

    return pl.pallas_call(
        functools.partial(_tail_kernel, alpha),
        grid=(m // tm,),
        in_specs=[act(ya), act(yb), act(gates), act(x), act(p), const(lig), const(lib),
                  const(wa), const(wb), const(wo), const(wpg), const(bpg), const(wpe),
                  const(lg), const(lb)],
        out_specs=pl.BlockSpec((tm, d), lambda i: (i, 0)),
        out_shape=jax.ShapeDtypeStruct((m, d), F32),
        compiler_params=pltpu.CompilerParams(
            dimension_semantics=("parallel",), vmem_limit_bytes=VMEM_LIMIT),
        name="tail",
    )(ya, yb, gates, x, p, lig, lib, wa, wb, wo, wpg, bpg, wpe, lg, lb)


def kernel(x, p, ln_in_g, ln_in_b, w_in, hg_lb_logits, hg_norm_g, w_merge, b_merge,
           w_br_hg, w_br_sb, w_out, w_pe, w_pg, b_pg, ln_g, ln_b):
    batch, seq, d = x.shape
    depth = w_in.shape[0]
    assert depth == 1, "kernel is written for a single layer"
    hg_w = w_br_hg.shape[1]
    sb_w = w_br_sb.shape[1]
    assert w_in.shape[2] == 4 * hg_w + 4 * sb_w
    hg_heads, sb_heads = hg_w // HEAD, sb_w // HEAD
    alpha = float((2 * depth) ** 0.25)
    m = batch * seq

    x2 = x.reshape(m, d)
    p2 = p[0].reshape(m, p.shape[-1])
    row = lambda a: a.reshape(1, -1)

    w = w_in[0]
    scale = HEAD ** -0.5
    w_hf = w[:, hg_w:2 * hg_w].astype(BF16)
    sq0 = 4 * hg_w
    w_bf = jnp.concatenate(
        [w[:, :hg_w], w[:, 2 * hg_w:sq0], w[:, sq0:sq0 + sb_w] * scale, w[:, sq0 + sb_w:]],
        axis=1).astype(BF16)

    h = _ln_call(x2, row(ln_in_g), row(ln_in_b))
    pbf = _matmul_call(h, w_bf, None, BF16, "proj_bf16")
    pf = _matmul_call(h, w_hf, None, F32, "proj_f32")
    gates = _matmul_call(h, w_merge[0].astype(BF16), row(b_merge[0]), BF16, "merge_gates")

    ya = _hgrn2_call(pbf, pf, hg_lb_logits, row(hg_norm_g[0]), batch, seq, hg_heads)

    idx = jnp.arange(SB_BLOCK)
    later = (idx[:, None] > idx[None, :]).astype(BF16)
    later2 = jnp.concatenate([later, later], axis=0)
    yb = _sb_call(pbf, later2, batch, seq, sb_heads, 3 * hg_w)

    out = _tail_call(alpha, ya, yb, gates, x2, p2, row(ln_in_g), row(ln_in_b),
                     w_br_hg[0].astype(BF16), w_br_sb[0].astype(BF16), w_out[0].astype(BF16),
                     w_pg[0].astype(BF16), row(b_pg[0]), w_pe[0].astype(BF16),
                     row(ln_g[0]), row(ln_b[0]))
    return out.reshape(batch, seq, d)
```

```python
import functools

import jax
import jax.numpy as jnp
from jax import lax
from jax.experimental import pallas as pl
from jax.experimental.pallas import tpu as pltpu

LN_EPS = 1e-5
HEAD = 128
HG_CHUNK = 128
HG_HEADS_PER_STEP = 2
HG_ROWS_PER_STEP = 1024
SB_BLOCK = 256
SB_EXP_ZERO = 104.0
VMEM_LIMIT = 52 * 1024 * 1024

F32 = jnp.float32
BF16 = jnp.bfloat16


def _dot(a, b):
    return jnp.dot(a, b, preferred_element_type=F32)


def _dot_nt(a, b):
    return lax.dot_general(a, b, (((1,), (1,)), ((), ())), preferred_element_type=F32)


def _layer_norm(x, g, b):
    mu = jnp.mean(x, axis=-1, keepdims=True)
    xc = x - mu
    var = jnp.mean(xc * xc, axis=-1, keepdims=True)
    return xc * lax.rsqrt(var + LN_EPS) * g + b


def _sigmoid_tanh(x):
    return 0.5 * jnp.tanh(0.5 * x) + 0.5


def _ln_kernel(x_ref, g_ref, b_ref, o_ref):
    o_ref[...] = _layer_norm(x_ref[...], g_ref[...], b_ref[...]).astype(o_ref.dtype)


def _ln_call(x, g, b, tm=512):
    m, d = x.shape
    return pl.pallas_call(
        _ln_kernel,
        grid=(m // tm,),
        in_specs=[pl.BlockSpec((tm, d), lambda i: (i, 0)),
                  pl.BlockSpec((1, d), lambda i: (0, 0)),
                  pl.BlockSpec((1, d), lambda i: (0, 0))],
        out_specs=pl.BlockSpec((tm, d), lambda i: (i, 0)),
        out_shape=jax.ShapeDtypeStruct((m, d), BF16),
        compiler_params=pltpu.CompilerParams(
            dimension_semantics=("parallel",), vmem_limit_bytes=VMEM_LIMIT),
        name="ln_in",
    )(x, g, b)


def _matmul_kernel(h_ref, w_ref, o_ref):
    o_ref[...] = _dot(h_ref[...], w_ref[...]).astype(o_ref.dtype)


def _matmul_gate_kernel(h_ref, w_ref, b_ref, o_ref):
    o_ref[...] = _sigmoid_tanh(_dot(h_ref[...], w_ref[...]) + b_ref[...]).astype(o_ref.dtype)


def _matmul_call(h, w, bias, out_dtype, name, n_blocks, w_col=lambda j: j, tm=1024, tn=1024):
    m, k = h.shape
    n = n_blocks * tn
    tm = min(tm, m)
    in_specs = [pl.BlockSpec((tm, k), lambda i, j: (i, 0)),
                pl.BlockSpec((k, tn), lambda i, j: (0, w_col(j)))]
    args = [h, w]
    kern = _matmul_kernel
    if bias is not None:
        in_specs.append(pl.BlockSpec((1, tn), lambda i, j: (0, j)))
        args.append(bias)
        kern = _matmul_gate_kernel
    return pl.pallas_call(
        kern,
        grid=(m // tm, n // tn),
        in_specs=in_specs,
        out_specs=pl.BlockSpec((tm, tn), lambda i, j: (i, j)),
        out_shape=jax.ShapeDtypeStruct((m, n), out_dtype),
        compiler_params=pltpu.CompilerParams(
            dimension_semantics=("parallel", "parallel"), vmem_limit_bytes=VMEM_LIMIT),
        name=name,
    )(*args)


def _hgrn2_kernel(q_ref, f_ref, i_ref, g_ref, lbl_ref, ng_ref, o_ref, st_ref):
    c = HG_CHUNK
    n_chunks = q_ref.shape[0] // c
    n_heads = q_ref.shape[1] // HEAD

    @pl.when(pl.program_id(2) == 0)
    def _():
        st_ref[...] = jnp.zeros_like(st_ref)

    lbl = lbl_ref[...]
    e = jnp.exp(lbl - jnp.max(lbl, axis=0, keepdims=True))
    lb_all = e[0:1, :] / jnp.sum(e, axis=0, keepdims=True)

    row = lax.broadcasted_iota(jnp.int32, (c, c), 0)
    col = lax.broadcasted_iota(jnp.int32, (c, c), 1)
    causal = col <= row
    tril = jnp.where(causal, 1.0, 0.0).astype(BF16)
    tril2 = jnp.concatenate([tril, tril], axis=1)

    heads = [slice(hh * HEAD, (hh + 1) * HEAD) for hh in range(n_heads)]
    chunks = [slice(ci * c, (ci + 1) * c) for ci in range(n_chunks)]
    one_m_lb = 1.0 - lb_all

    def decays(rs):
        f = lb_all + one_m_lb * _sigmoid_tanh(f_ref[rs, :])
        lf2 = jnp.log2(f)
        hi = lf2.astype(BF16)
        lo = (lf2 - hi.astype(F32)).astype(BF16)
        return 1.0 - f, _dot(tril2, jnp.concatenate([hi, lo], axis=0))

    def operands(rs, cs, kk, bc):
        b_last = bc[c - 1:c, :]
        b_mid = bc[c // 2 - 1:c // 2, :]
        q = q_ref[rs, cs].astype(F32)
        q_dec = (q * jnp.exp2(bc)).astype(BF16)
        q_n = (q * jnp.exp2(bc - b_mid)).astype(BF16)
        k_mid = kk * jnp.exp2(b_mid - bc)
        k_end = (k_mid * jnp.exp2(b_last - b_mid)).astype(BF16)
        a = jnp.where(causal, _dot_nt(q_n, k_mid.astype(BF16)), 0.0).astype(BF16)
        v_t = i_ref[rs, cs].astype(F32).T.astype(BF16)
        return a, q_dec, k_end, v_t, jnp.exp2(b_last)

    def recur(rs, cs, st, a, q_dec, k_end, v_t, dec):
        o = _dot_nt(jnp.concatenate([a, q_dec], axis=1),
                    jnp.concatenate([v_t, st.astype(BF16)], axis=1))
        ms = jnp.mean(o * o, axis=-1, keepdims=True)
        g = g_ref[rs, cs].astype(F32)
        y = o * lax.rsqrt(ms + LN_EPS) * ng_ref[:, cs] * (g * _sigmoid_tanh(g))
        o_ref[rs, cs] = y.astype(o_ref.dtype)
        return st * dec + _dot(v_t, k_end)

    states = [st_ref[hh] for hh in range(n_heads)]
    prep, ops = {}, {}
    for step in range(n_chunks + 2):
        if step < n_chunks:
            prep[step] = decays(chunks[step])
        ci = step - 1
        if 0 <= ci < n_chunks:
            kk_all, bc_all = prep.pop(ci)
            for hh, cs in enumerate(heads):
                ops[ci, hh] = operands(chunks[ci], cs, kk_all[:, cs], bc_all[:, cs])
        ci = step - 2
        if 0 <= ci < n_chunks:
            for hh, cs in enumerate(heads):
                states[hh] = recur(chunks[ci], cs, states[hh], *ops.pop((ci, hh)))
    for hh in range(n_heads):
        st_ref[hh] = states[hh]


def _hgrn2_call(pbf, pf, lb_logits, norm_g, batch, seq, n_heads):
    m = batch * seq
    hb, tt = HG_HEADS_PER_STEP, min(HG_ROWS_PER_STEP, seq)
    w = hb * HEAD
    nt = seq // tt
    n_hg = n_heads // hb
    sec = n_heads * HEAD // w

    def rows(b, h, t):
        return b * nt + t

    return pl.pallas_call(
        _hgrn2_kernel,
        grid=(batch, n_hg, nt),
        in_specs=[pl.BlockSpec((tt, w), lambda b, h, t: (rows(b, h, t), h)),
                  pl.BlockSpec((tt, w), lambda b, h, t: (rows(b, h, t), h)),
                  pl.BlockSpec((tt, w), lambda b, h, t: (rows(b, h, t), sec + h)),
                  pl.BlockSpec((tt, w), lambda b, h, t: (rows(b, h, t), 2 * sec + h)),
                  pl.BlockSpec((2, w), lambda b, h, t: (0, h)),
                  pl.BlockSpec((1, w), lambda b, h, t: (0, h))],
        out_specs=pl.BlockSpec((tt, w), lambda b, h, t: (rows(b, h, t), h)),
        out_shape=jax.ShapeDtypeStruct((m, n_heads * HEAD), BF16),
        scratch_shapes=[pltpu.VMEM((hb, HEAD, HEAD), F32)],
        compiler_params=pltpu.CompilerParams(
            dimension_semantics=("parallel", "parallel", "arbitrary"),
            vmem_limit_bytes=VMEM_LIMIT),
        name="hgrn2",
    )(pbf, pf, pbf, pbf, lb_logits, norm_g)


def _softplus(z):
    return jnp.maximum(z, 0.0) + jnp.log(1.0 + jnp.exp(-jnp.abs(z)))


def _sb_scores(q, k, diagonal):
    blk = q.shape[0]
    z = _dot_nt(q, k)
    sp = _softplus(z)
    log_beta = z - sp
    strict = None
    if diagonal:
        qrow = lax.broadcasted_iota(jnp.int32, (blk, blk), 0)
        kcol = lax.broadcasted_iota(jnp.int32, (blk, blk), 1)
        strict = kcol < qrow
        sp = jnp.where(strict, sp, 0.0)
    return log_beta, sp.astype(BF16), jnp.sum(sp, axis=-1, keepdims=True), strict


def _sb_weights(log_beta, sp_b, later, s, strict):
    blk = log_beta.shape[0]
    log_a = log_beta - _dot(sp_b, later)
    if s is not None:
        log_a = log_a - jnp.concatenate([s] * (blk // HEAD), axis=1)
    a = jnp.exp(log_a)
    if strict is not None:
        a = jnp.where(strict, a, 0.0)
    return a.astype(BF16)


def _sb_sweep(units, later, acc, s):
    n = len(units)
    st1, st2 = {}, {}
    for t in range(n + 2):
        if t < n:
            _, q, k, _, diagonal = units[t]
            st1[t] = _sb_scores(q, k, diagonal)
        u = t - 1
        if 0 <= u < n:
            h = units[u][0]
            log_beta, sp_b, rowsum, strict = st1.pop(u)
            st2[u] = _sb_weights(log_beta, sp_b, later, s.get(h), strict)
            s[h] = s[h] + rowsum if h in s else jnp.broadcast_to(rowsum, (rowsum.shape[0], HEAD))
        u = t - 2
        if 0 <= u < n:
            h, v = units[u][0], units[u][3]
            part = _dot(st2.pop(u), v)
            acc[h] = acc[h] + part if h in acc else part


def _sb_kernel(k_col0, seq, q_ref, kd_ref, vd_ref, kp_ref, vp_ref, g_ref, later_ref, kv_hbm,
               o_ref, acc_ref, s_ref, kbuf, vbuf, sem):
    blk = SB_BLOCK
    b = pl.program_id(0)
    qi = pl.program_id(1)
    width = q_ref.shape[1]
    n_heads = width // HEAD
    heads = [slice(h * HEAD, (h + 1) * HEAD) for h in range(n_heads)]
    later = later_ref[...]
    has_prev = qi > 0

    units = []
    for h, cs in enumerate(heads):
        q = q_ref[:, cs]
        units.append((h, q, kd_ref[:, cs], vd_ref[:, cs], True))
        vp = jnp.where(has_prev, vp_ref[:, cs], jnp.zeros((blk, HEAD), BF16))
        units.append((h, q, kp_ref[:, cs], vp, False))
    acc, s = {}, {}
    _sb_sweep(units, later, acc, s)
    s_min = None
    for h in range(n_heads):
        acc_ref[h] = acc[h]
        s_ref[h] = s[h]
        s_min = s[h] if s_min is None else jnp.minimum(s_min, s[h])

    def cond(carry):
        kb, s_low = carry
        return jnp.logical_and(kb >= 0, s_low < SB_EXP_ZERO)

    def body(carry):
        kb, _ = carry
        row0 = pl.multiple_of(b * seq + kb * blk, blk)
        ck = pltpu.make_async_copy(
            kv_hbm.at[pl.ds(row0, blk), pl.ds(k_col0, width)], kbuf, sem.at[0])
        cv = pltpu.make_async_copy(
            kv_hbm.at[pl.ds(row0, blk), pl.ds(k_col0 + width, width)], vbuf, sem.at[1])
        ck.start()
        cv.start()
        ck.wait()
        cv.wait()
        units = [(h, q_ref[:, cs], kbuf[:, cs], vbuf[:, cs], False) for h, cs in enumerate(heads)]
        acc = {h: acc_ref[h] for h in range(n_heads)}
        s = {h: s_ref[h] for h in range(n_heads)}
        _sb_sweep(units, later, acc, s)
        s_low = None
        for h in range(n_heads):
            acc_ref[h] = acc[h]
            s_ref[h] = s[h]
            s_low = s[h] if s_low is None else jnp.minimum(s_low, s[h])
        return kb - 1, jnp.min(s_low)

    lax.while_loop(cond, body, (qi - 2, jnp.min(s_min)))

    for h, cs in enumerate(heads):
        g = g_ref[:, cs].astype(F32)
        o_ref[:, cs] = (acc_ref[h] * (g * _sigmoid_tanh(g))).astype(o_ref.dtype)


def _sb_call(pbf, later, batch, seq, n_heads, col0):
    m = batch * seq
    blk = SB_BLOCK
    nq = seq // blk
    width = n_heads * HEAD
    c0 = col0 // width

    def tile(sec):
        return pl.BlockSpec((blk, width), lambda b, i: (b * nq + i, c0 + sec))

    def prev(sec):
        return pl.BlockSpec((blk, width), lambda b, i: (b * nq + jnp.maximum(i - 1, 0), c0 + sec))

    return pl.pallas_call(
        functools.partial(_sb_kernel, col0 + width, seq),
        grid=(batch, nq),
        in_specs=[tile(0), tile(1), tile(2), prev(1), prev(2), tile(3),
                  pl.BlockSpec((blk, blk), lambda b, i: (0, 0)),
                  pl.BlockSpec(memory_space=pl.ANY)],
        out_specs=pl.BlockSpec((blk, width), lambda b, i: (b * nq + i, 0)),
        out_shape=jax.ShapeDtypeStruct((m, width), BF16),
        scratch_shapes=[pltpu.VMEM((n_heads, blk, HEAD), F32),
                        pltpu.VMEM((n_heads, blk, HEAD), F32),
                        pltpu.VMEM((blk, width), BF16),
                        pltpu.VMEM((blk, width), BF16),
                        pltpu.SemaphoreType.DMA((2,))],
        compiler_params=pltpu.CompilerParams(
            dimension_semantics=("parallel", "arbitrary"),
            vmem_limit_bytes=VMEM_LIMIT),
        name="stick_breaking",
    )(pbf, pbf, pbf, pbf, pbf, pbf, later, pbf)


def _tail_kernel(alpha, ya_ref, yb_ref, gt_ref, x_ref, p_ref, lig_ref, lib_ref,
                 wa_ref, wb_ref, wo_ref, wpg_ref, bpg_ref, wpe_ref, lg_ref, lb_ref, o_ref):
    d = o_ref.shape[1]
    ma = _dot(ya_ref[...], wa_ref[...])
    mb = _dot(yb_ref[...], wb_ref[...])
    mm = gt_ref[:, :d].astype(F32) * ma + gt_ref[:, d:].astype(F32) * mb
    h = _layer_norm(x_ref[...], lig_ref[...], lib_ref[...])
    r = alpha * h + _dot(mm.astype(BF16), wo_ref[...])
    pg = _sigmoid_tanh(_dot(r.astype(BF16), wpg_ref[...]) + bpg_ref[...])
    pe = _dot(p_ref[...].astype(BF16), wpe_ref[...])
    r = r + pg * pe
    o_ref[...] = _layer_norm(r, lg_ref[...], lb_ref[...])


def _tail_call(alpha, ya, yb, gates, x, p, lig, lib, wa, wb, wo, wpg, bpg, wpe, lg, lb, tm=256):
    m, d = x.shape
    tm = min(tm, m)

    def act(a):
        return pl.BlockSpec((tm, a.shape[1]), lambda i: (i, 0))

    def const(a):
        return pl.BlockSpec(a.shape, lambda i: (0, 0), pipeline_mode=pl.Buffered(1))

    return pl.pallas_call(
        functools.partial(_tail_kernel, alpha),
        grid=(m // tm,),
        in_specs=[act(ya), act(yb), act(gates), act(x), act(p), const(lig), const(lib),
                  const(wa), const(wb), const(wo), const(wpg), const(bpg), const(wpe),
                  const(lg), const(lb)],
        out_specs=pl.BlockSpec((tm, d), lambda i: (i, 0)),
        out_shape=jax.ShapeDtypeStruct((m, d), F32),
        compiler_params=pltpu.CompilerParams(
            dimension_semantics=("parallel",), vmem_limit_bytes=VMEM_LIMIT),
        name="tail",
    )(ya, yb, gates, x, p, lig, lib, wa, wb, wo, wpg, bpg, wpe, lg, lb)


def kernel(x, p, ln_in_g, ln_in_b, w_in, hg_lb_logits, hg_norm_g, w_merge, b_merge,
           w_br_hg, w_br_sb, w_out, w_pe, w_pg, b_pg, ln_g, ln_b):
    batch, seq, d = x.shape
    depth = w_in.shape[0]
    assert depth == 1, "kernel is written for a single layer"
    hg_w = w_br_hg.shape[1]
    sb_w = w_br_sb.shape[1]
    assert w_in.shape[2] == 4 * hg_w + 4 * sb_w
    hg_heads, sb_heads = hg_w // HEAD, sb_w // HEAD
    alpha = float((2 * depth) ** 0.25)
    m = batch * seq

    x2 = x.reshape(m, d)
    p2 = p[0].reshape(m, p.shape[-1])
    row = lambda a: a.reshape(1, -1)

    sq0 = 4 * hg_w
    col = jnp.arange(w_in.shape[2])
    col_scale = jnp.where((col >= sq0) & (col < sq0 + sb_w), HEAD ** -0.5, 1.0).astype(F32)
    w_all = (w_in[0] * col_scale).astype(BF16)
    tn = 1024
    hf0, n_hf = hg_w // tn, hg_w // tn
    n_bf = w_in.shape[2] // tn - n_hf

    h = _ln_call(x2, row(ln_in_g), row(ln_in_b))
    pbf = _matmul_call(h, w_all, None, BF16, "proj_bf16", n_bf,
                       lambda j: j + jnp.where(j >= hf0, n_hf, 0), tn=tn)
    pf = _matmul_call(h, w_all, None, F32, "proj_f32", n_hf, lambda j: j + hf0, tn=tn)
    gates = _matmul_call(h, w_merge[0].astype(BF16), row(b_merge[0]), BF16, "merge_gates",
                         w_merge.shape[2] // tn, tn=tn)

    ya = _hgrn2_call(pbf, pf, hg_lb_logits, row(hg_norm_g[0]), batch, seq, hg_heads)

    idx = jnp.arange(SB_BLOCK)
    later = (idx[:, None] > idx[None, :]).astype(BF16)
    yb = _sb_call(pbf, later, batch, seq, sb_heads, 3 * hg_w)

    out = _tail_call(alpha, ya, yb, gates, x2, p2, row(ln_in_g), row(ln_in_b),
                     w_br_hg[0].astype(BF16), w_br_sb[0].astype(BF16), w_out[0].astype(BF16),
                     w_pg[0].astype(BF16), row(b_pg[0]), w_pe[0].astype(BF16),
                     row(ln_g[0]), row(ln_b[0]))
    return out.reshape(batch, seq, d)
```

```python
import functools

import jax
import jax.numpy as jnp
from jax import lax
from jax.experimental import pallas as pl
from jax.experimental.pallas import tpu as pltpu

LN_EPS = 1e-5
LOG2_E = 1.4426950408889634
HEAD = 128
HG_CHUNK = 128
HG_HEADS_PER_STEP = 4
HG_ROWS_PER_STEP = 1024
SB_BLOCK = 256
SB_EXP_ZERO = 104.0
VMEM_LIMIT = 52 * 1024 * 1024

F32 = jnp.float32
BF16 = jnp.bfloat16


def _dot(a, b):
    return jnp.dot(a, b, preferred_element_type=F32)


def _dot_nt(a, b):
    return lax.dot_general(a, b, (((1,), (1,)), ((), ())), preferred_element_type=F32)


def _layer_norm(x, g, b):
    mu = jnp.mean(x, axis=-1, keepdims=True)
    xc = x - mu
    var = jnp.mean(xc * xc, axis=-1, keepdims=True)
    return xc * lax.rsqrt(var + LN_EPS) * g + b


def _sigmoid_tanh(x):
    return 0.5 * jnp.tanh(0.5 * x) + 0.5


def _silu(x):
    half = 0.5 * x
    return half * (jnp.tanh(half) + 1.0)


def _ln_kernel(x_ref, g_ref, b_ref, o_ref):
    o_ref[...] = _layer_norm(x_ref[...], g_ref[...], b_ref[...]).astype(o_ref.dtype)


def _ln_call(x, g, b, tm=512):
    m, d = x.shape
    return pl.pallas_call(
        _ln_kernel,
        grid=(m // tm,),
        in_specs=[pl.BlockSpec((tm, d), lambda i: (i, 0)),
                  pl.BlockSpec((1, d), lambda i: (0, 0)),
                  pl.BlockSpec((1, d), lambda i: (0, 0))],
        out_specs=pl.BlockSpec((tm, d), lambda i: (i, 0)),
        out_shape=jax.ShapeDtypeStruct((m, d), BF16),
        compiler_params=pltpu.CompilerParams(
            dimension_semantics=("parallel",), vmem_limit_bytes=VMEM_LIMIT),
        name="ln_in",
    )(x, g, b)


def _matmul_kernel(h_ref, w_ref, o_ref):
    o_ref[...] = _dot(h_ref[...], w_ref[...]).astype(o_ref.dtype)


def _matmul_gate_kernel(h_ref, w_ref, b_ref, o_ref):
    o_ref[...] = _sigmoid_tanh(_dot(h_ref[...], w_ref[...]) + b_ref[...]).astype(o_ref.dtype)


def _matmul_call(h, w, bias, out_dtype, name, n_blocks, w_col=lambda j: j, tm=1024, tn=1024):
    m, k = h.shape
    n = n_blocks * tn
    tm = min(tm, m)
    in_specs = [pl.BlockSpec((tm, k), lambda i, j: (i, 0)),
                pl.BlockSpec((k, tn), lambda i, j: (0, w_col(j)))]
    args = [h, w]
    kern = _matmul_kernel
    if bias is not None:
        in_specs.append(pl.BlockSpec((1, tn), lambda i, j: (0, j)))
        args.append(bias)
        kern = _matmul_gate_kernel
    return pl.pallas_call(
        kern,
        grid=(m // tm, n // tn),
        in_specs=in_specs,
        out_specs=pl.BlockSpec((tm, tn), lambda i, j: (i, j)),
        out_shape=jax.ShapeDtypeStruct((m, n), out_dtype),
        compiler_params=pltpu.CompilerParams(
            dimension_semantics=("parallel", "parallel"), vmem_limit_bytes=VMEM_LIMIT),
        name=name,
    )(*args)


def _hgrn2_kernel(q_ref, f_ref, i_ref, g_ref, lbl_ref, ng_ref, o_ref, st_ref):
    c = HG_CHUNK
    n_chunks = q_ref.shape[0] // c
    n_heads = q_ref.shape[1] // HEAD

    @pl.when(pl.program_id(2) == 0)
    def _():
        st_ref[...] = jnp.zeros_like(st_ref)

    lbl = lbl_ref[...]
    e = jnp.exp(lbl - jnp.max(lbl, axis=0, keepdims=True))
    lb_all = e[0:1, :] / jnp.sum(e, axis=0, keepdims=True)

    row = lax.broadcasted_iota(jnp.int32, (c, c), 0)
    col = lax.broadcasted_iota(jnp.int32, (c, c), 1)
    causal = col <= row
    tril = jnp.where(causal, 1.0, 0.0).astype(BF16)
    tril2 = jnp.concatenate([tril, tril], axis=1)

    heads = [slice(hh * HEAD, (hh + 1) * HEAD) for hh in range(n_heads)]
    chunks = [slice(ci * c, (ci + 1) * c) for ci in range(n_chunks)]
    f_half = 0.5 * (1.0 - lb_all)
    f_mid = lb_all + f_half

    def decays(rs):
        f = f_mid + f_half * jnp.tanh(0.5 * f_ref[rs, :])
        lf2 = jnp.log2(f)
        hi = lf2.astype(BF16)
        lo = (lf2 - hi.astype(F32)).astype(BF16)
        return 1.0 - f, _dot(tril2, jnp.concatenate([hi, lo], axis=0))

    def operands(rs, cs, kk, bc):
        b_last = bc[c - 1:c, :]
        b_mid = bc[c // 2 - 1:c // 2, :]
        q = q_ref[rs, cs].astype(F32)
        q_dec = (q * jnp.exp2(bc)).astype(BF16)
        q_n = (q * jnp.exp2(bc - b_mid)).astype(BF16)
        k_mid = kk * jnp.exp2(b_mid - bc)
        k_end = (k_mid * jnp.exp2(b_last - b_mid)).astype(BF16)
        a = jnp.where(causal, _dot_nt(q_n, k_mid.astype(BF16)), 0.0).astype(BF16)
        v_t = i_ref[rs, cs].astype(F32).T.astype(BF16)
        return a, q_dec, k_end, v_t, jnp.exp2(b_last)

    def recur(rs, cs, st, a, q_dec, k_end, v_t, dec):
        o = _dot_nt(jnp.concatenate([a, q_dec], axis=1),
                    jnp.concatenate([v_t, st.astype(BF16)], axis=1))
        ms = jnp.mean(o * o, axis=-1, keepdims=True)
        g = g_ref[rs, cs].astype(F32)
        y = o * lax.rsqrt(ms + LN_EPS) * ng_ref[:, cs] * _silu(g)
        o_ref[rs, cs] = y.astype(o_ref.dtype)
        return st * dec + _dot(v_t, k_end)

    states = [st_ref[hh] for hh in range(n_heads)]
    prep, ops = {}, {}
    for step in range(n_chunks + 2):
        if step < n_chunks:
            prep[step] = decays(chunks[step])
        ci = step - 1
        if 0 <= ci < n_chunks:
            kk_all, bc_all = prep.pop(ci)
            for hh, cs in enumerate(heads):
                ops[ci, hh] = operands(chunks[ci], cs, kk_all[:, cs], bc_all[:, cs])
        ci = step - 2
        if 0 <= ci < n_chunks:
            for hh, cs in enumerate(heads):
                states[hh] = recur(chunks[ci], cs, states[hh], *ops.pop((ci, hh)))
    for hh in range(n_heads):
        st_ref[hh] = states[hh]


def _hgrn2_call(pbf, pf, lb_logits, norm_g, batch, seq, n_heads):
    m = batch * seq
    hb, tt = HG_HEADS_PER_STEP, min(HG_ROWS_PER_STEP, seq)
    w = hb * HEAD
    nt = seq // tt
    n_hg = n_heads // hb
    sec = n_heads * HEAD // w

    def rows(b, h, t):
        return b * nt + t

    return pl.pallas_call(
        _hgrn2_kernel,
        grid=(batch, n_hg, nt),
        in_specs=[pl.BlockSpec((tt, w), lambda b, h, t: (rows(b, h, t), h)),
                  pl.BlockSpec((tt, w), lambda b, h, t: (rows(b, h, t), h)),
                  pl.BlockSpec((tt, w), lambda b, h, t: (rows(b, h, t), sec + h)),
                  pl.BlockSpec((tt, w), lambda b, h, t: (rows(b, h, t), 2 * sec + h)),
                  pl.BlockSpec((2, w), lambda b, h, t: (0, h)),
                  pl.BlockSpec((1, w), lambda b, h, t: (0, h))],
        out_specs=pl.BlockSpec((tt, w), lambda b, h, t: (rows(b, h, t), h)),
        out_shape=jax.ShapeDtypeStruct((m, n_heads * HEAD), BF16),
        scratch_shapes=[pltpu.VMEM((hb, HEAD, HEAD), F32)],
        compiler_params=pltpu.CompilerParams(
            dimension_semantics=("parallel", "parallel", "arbitrary"),
            vmem_limit_bytes=VMEM_LIMIT),
        name="hgrn2",
    )(pbf, pf, pbf, pbf, lb_logits, norm_g)


def _softplus(z):
    return jnp.maximum(z, 0.0) + jnp.log(1.0 + jnp.exp2(jnp.abs(z) * -LOG2_E))


def _sb_scores(q, k, diagonal):
    blk = q.shape[0]
    z = _dot_nt(q, k)
    sp = _softplus(z)
    log_beta = z - sp
    strict = None
    if diagonal:
        qrow = lax.broadcasted_iota(jnp.int32, (blk, blk), 0)
        kcol = lax.broadcasted_iota(jnp.int32, (blk, blk), 1)
        strict = kcol < qrow
        sp = jnp.where(strict, sp, 0.0)
    return log_beta, sp.astype(BF16), jnp.sum(sp, axis=-1, keepdims=True), strict


def _sb_weights(log_beta, sp_b, later, s, strict):
    blk = log_beta.shape[0]
    log_a = log_beta - _dot(sp_b, later)
    if s is not None:
        log_a = log_a - jnp.concatenate([s] * (blk // HEAD), axis=1)
    a = jnp.exp(log_a)
    if strict is not None:
        a = jnp.where(strict, a, 0.0)
    return a.astype(BF16)


def _sb_sweep(units, later, acc, s):
    n = len(units)
    st1, st2 = {}, {}
    for t in range(n + 2):
        if t < n:
            _, q, k, _, diagonal = units[t]
            st1[t] = _sb_scores(q, k, diagonal)
        u = t - 1
        if 0 <= u < n:
            h = units[u][0]
            log_beta, sp_b, rowsum, strict = st1.pop(u)
            st2[u] = _sb_weights(log_beta, sp_b, later, s.get(h), strict)
            s[h] = s[h] + rowsum if h in s else jnp.broadcast_to(rowsum, (rowsum.shape[0], HEAD))
        u = t - 2
        if 0 <= u < n:
            h, v = units[u][0], units[u][3]
            part = _dot(st2.pop(u), v)
            acc[h] = acc[h] + part if h in acc else part


def _sb_kernel(k_col0, seq, q_ref, kd_ref, vd_ref, kp_ref, vp_ref, g_ref, later_ref, kv_hbm,
               o_ref, acc_ref, s_ref, kbuf, vbuf, sem):
    blk = SB_BLOCK
    b = pl.program_id(0)
    qi = pl.program_id(1)
    width = q_ref.shape[1]
    n_heads = width // HEAD
    heads = [slice(h * HEAD, (h + 1) * HEAD) for h in range(n_heads)]
    later = later_ref[...]
    has_prev = qi > 0

    units = []
    for h, cs in enumerate(heads):
        q = q_ref[:, cs]
        units.append((h, q, kd_ref[:, cs], vd_ref[:, cs], True))
        vp = jnp.where(has_prev, vp_ref[:, cs], jnp.zeros((blk, HEAD), BF16))
        units.append((h, q, kp_ref[:, cs], vp, False))
    acc, s = {}, {}
    _sb_sweep(units, later, acc, s)
    s_min = None
    for h in range(n_heads):
        acc_ref[h] = acc[h]
        s_ref[h] = s[h]
        s_min = s[h] if s_min is None else jnp.minimum(s_min, s[h])

    def cond(carry):
        kb, s_low = carry
        return jnp.logical_and(kb >= 0, s_low < SB_EXP_ZERO)

    def body(carry):
        kb, _ = carry
        row0 = pl.multiple_of(b * seq + kb * blk, blk)
        ck = pltpu.make_async_copy(
            kv_hbm.at[pl.ds(row0, blk), pl.ds(k_col0, width)], kbuf, sem.at[0])
        cv = pltpu.make_async_copy(
            kv_hbm.at[pl.ds(row0, blk), pl.ds(k_col0 + width, width)], vbuf, sem.at[1])
        ck.start()
        cv.start()
        ck.wait()
        cv.wait()
        units = [(h, q_ref[:, cs], kbuf[:, cs], vbuf[:, cs], False) for h, cs in enumerate(heads)]
        acc = {h: acc_ref[h] for h in range(n_heads)}
        s = {h: s_ref[h] for h in range(n_heads)}
        _sb_sweep(units, later, acc, s)
        s_low = None
        for h in range(n_heads):
            acc_ref[h] = acc[h]
            s_ref[h] = s[h]
            s_low = s[h] if s_low is None else jnp.minimum(s_low, s[h])
        return kb - 1, jnp.min(s_low)

    lax.while_loop(cond, body, (qi - 2, jnp.min(s_min)))

    for h, cs in enumerate(heads):
        g = g_ref[:, cs].astype(F32)
        o_ref[:, cs] = (acc_ref[h] * _silu(g)).astype(o_ref.dtype)


def _sb_call(pbf, later, batch, seq, n_heads, col0):
    m = batch * seq
    blk = SB_BLOCK
    nq = seq // blk
    width = n_heads * HEAD
    c0 = col0 // width

    def tile(sec):
        return pl.BlockSpec((blk, width), lambda b, i: (b * nq + i, c0 + sec))

    def prev(sec):
        return pl.BlockSpec((blk, width), lambda b, i: (b * nq + jnp.maximum(i - 1, 0), c0 + sec))

    return pl.pallas_call(
        functools.partial(_sb_kernel, col0 + width, seq),
        grid=(batch, nq),
        in_specs=[tile(0), tile(1), tile(2), prev(1), prev(2), tile(3),
                  pl.BlockSpec((blk, blk), lambda b, i: (0, 0)),
                  pl.BlockSpec(memory_space=pl.ANY)],
        out_specs=pl.BlockSpec((blk, width), lambda b, i: (b * nq + i, 0)),
        out_shape=jax.ShapeDtypeStruct((m, width), BF16),
        scratch_shapes=[pltpu.VMEM((n_heads, blk, HEAD), F32),
                        pltpu.VMEM((n_heads, blk, HEAD), F32),
                        pltpu.VMEM((blk, width), BF16),
                        pltpu.VMEM((blk, width), BF16),
                        pltpu.SemaphoreType.DMA((2,))],
        compiler_params=pltpu.CompilerParams(
            dimension_semantics=("parallel", "arbitrary"),
            vmem_limit_bytes=VMEM_LIMIT),
        name="stick_breaking",
    )(pbf, pbf, pbf, pbf, pbf, pbf, later, pbf)


def _tail_kernel(alpha, ya_ref, yb_ref, gt_ref, x_ref, p_ref, lig_ref, lib_ref,
                 wa_ref, wb_ref, wo_ref, wpg_ref, bpg_ref, wpe_ref, lg_ref, lb_ref, o_ref):
    d = o_ref.shape[1]
    ma = _dot(ya_ref[...], wa_ref[...])
    mb = _dot(yb_ref[...], wb_ref[...])
    mm = gt_ref[:, :d].astype(F32) * ma + gt_ref[:, d:].astype(F32) * mb
    h = _layer_norm(x_ref[...], lig_ref[...], lib_ref[...])
    r = alpha * h + _dot(mm.astype(BF16), wo_ref[...])
    pg = _sigmoid_tanh(_dot(r.astype(BF16), wpg_ref[...]) + bpg_ref[...])
    pe = _dot(p_ref[...].astype(BF16), wpe_ref[...])
    r = r + pg * pe
    o_ref[...] = _layer_norm(r, lg_ref[...], lb_ref[...])


def _tail_call(alpha, ya, yb, gates, x, p, lig, lib, wa, wb, wo, wpg, bpg, wpe, lg, lb, tm=256):
    m, d = x.shape
    tm = min(tm, m)

    def act(a):
        return pl.BlockSpec((tm, a.shape[1]), lambda i: (i, 0))

    def const(a):
        return pl.BlockSpec(a.shape, lambda i: (0, 0), pipeline_mode=pl.Buffered(1))

    return pl.pallas_call(
        functools.partial(_tail_kernel, alpha),
        grid=(m // tm,),
        in_specs=[act(ya), act(yb), act(gates), act(x), act(p), const(lig), const(lib),
                  const(wa), const(wb), const(wo), const(wpg), const(bpg), const(wpe),
                  const(lg), const(lb)],
        out_specs=pl.BlockSpec((tm, d), lambda i: (i, 0)),
        out_shape=jax.ShapeDtypeStruct((m, d), F32),
        compiler_params=pltpu.CompilerParams(
            dimension_semantics=("parallel",), vmem_limit_bytes=VMEM_LIMIT),
        name="tail",
    )(ya, yb, gates, x, p, lig, lib, wa, wb, wo, wpg, bpg, wpe, lg, lb)


def kernel(x, p, ln_in_g, ln_in_b, w_in, hg_lb_logits, hg_norm_g, w_merge, b_merge,
           w_br_hg, w_br_sb, w_out, w_pe, w_pg, b_pg, ln_g, ln_b):
    batch, seq, d = x.shape
    depth = w_in.shape[0]
    assert depth == 1, "kernel is written for a single layer"
    hg_w = w_br_hg.shape[1]
    sb_w = w_br_sb.shape[1]
    assert w_in.shape[2] == 4 * hg_w + 4 * sb_w
    hg_heads, sb_heads = hg_w // HEAD, sb_w // HEAD
    alpha = float((2 * depth) ** 0.25)
    m = batch * seq

    x2 = x.reshape(m, d)
    p2 = p[0].reshape(m, p.shape[-1])
    row = lambda a: a.reshape(1, -1)

    sq0 = 4 * hg_w
    col = jnp.arange(w_in.shape[2])
    col_scale = jnp.where((col >= sq0) & (col < sq0 + sb_w), HEAD ** -0.5, 1.0).astype(F32)
    w_all = (w_in[0] * col_scale).astype(BF16)
    tn = 2048
    hf0, n_hf = hg_w // tn, hg_w // tn
    n_bf = w_in.shape[2] // tn - n_hf

    h = _ln_call(x2, row(ln_in_g), row(ln_in_b))
    pbf = _matmul_call(h, w_all, None, BF16, "proj_bf16", n_bf,
                       lambda j: j + jnp.where(j >= hf0, n_hf, 0), tn=tn)
    pf = _matmul_call(h, w_all, None, F32, "proj_f32", n_hf, lambda j: j + hf0, tn=tn)
    gates = _matmul_call(h, w_merge[0].astype(BF16), row(b_merge[0]), BF16, "merge_gates",
                         w_merge.shape[2] // tn, tn=tn)

    ya = _hgrn2_call(pbf, pf, hg_lb_logits, row(hg_norm_g[0]), batch, seq, hg_heads)

    idx = jnp.arange(SB_BLOCK)
    later = (idx[:, None] > idx[None, :]).astype(BF16)
    yb = _sb_call(pbf, later, batch, seq, sb_heads, 3 * hg_w)

    out = _tail_call(alpha, ya, yb, gates, x2, p2, row(ln_in_g), row(ln_in_b),
                     w_br_hg[0].astype(BF16), w_br_sb[0].astype(BF16), w_out[0].astype(BF16),
                     w_pg[0].astype(BF16), row(b_pg[0]), w_pe[0].astype(BF16),
                     row(ln_g[0]), row(ln_b[0]))
    return out.reshape(batch, seq, d)
```

```python
import functools

import jax
import jax.numpy as jnp
from jax import lax
from jax.experimental import pallas as pl
from jax.experimental.pallas import tpu as pltpu

LN_EPS = 1e-5
LOG2_E = 1.4426950408889634
HEAD = 128
LN_SLAB = 256
HG_CHUNK = 128
HG_HEADS_PER_STEP = 4
HG_ROWS_PER_STEP = 1024
SB_BLOCK = 256
SB_EXP_ZERO = 104.0
VMEM_LIMIT = 52 * 1024 * 1024

F32 = jnp.float32
BF16 = jnp.bfloat16


def _dot(a, b):
    return jnp.dot(a, b, preferred_element_type=F32)


def _dot_nt(a, b):
    return lax.dot_general(a, b, (((1,), (1,)), ((), ())), preferred_element_type=F32)


def _layer_norm(x, g, b):
    mu = jnp.mean(x, axis=-1, keepdims=True)
    xc = x - mu
    var = jnp.mean(xc * xc, axis=-1, keepdims=True)
    return xc * lax.rsqrt(var + LN_EPS) * g + b


def _sigmoid_tanh(x):
    return 0.5 * jnp.tanh(0.5 * x) + 0.5


def _silu(x):
    half = 0.5 * x
    return half * (jnp.tanh(half) + 1.0)


def _ln_proj_kernel(x_ref, g_ref, b_ref, w_ref, h_ref, o_ref):
    for r0 in range(0, x_ref.shape[0], LN_SLAB):
        rs = slice(r0, r0 + LN_SLAB)
        h = _layer_norm(x_ref[rs, :], g_ref[...], b_ref[...]).astype(h_ref.dtype)
        h_ref[rs, :] = h
        o_ref[rs, :] = _dot(h, w_ref[...])


def _ln_proj_call(x, g, b, w, w_block, tn, tm=512):
    m, d = x.shape
    tm = min(tm, m)
    return pl.pallas_call(
        _ln_proj_kernel,
        grid=(m // tm,),
        in_specs=[pl.BlockSpec((tm, d), lambda i: (i, 0)),
                  pl.BlockSpec((1, d), lambda i: (0, 0)),
                  pl.BlockSpec((1, d), lambda i: (0, 0)),
                  pl.BlockSpec((d, tn), lambda i: (0, w_block), pipeline_mode=pl.Buffered(1))],
        out_specs=[pl.BlockSpec((tm, d), lambda i: (i, 0)),
                   pl.BlockSpec((tm, tn), lambda i: (i, 0))],
        out_shape=[jax.ShapeDtypeStruct((m, d), BF16),
                   jax.ShapeDtypeStruct((m, tn), F32)],
        compiler_params=pltpu.CompilerParams(
            dimension_semantics=("parallel",), vmem_limit_bytes=VMEM_LIMIT),
        name="ln_proj_f32",
    )(x, g, b, w)


def _matmul_kernel(h_ref, w_ref, o_ref):
    o_ref[...] = _dot(h_ref[...], w_ref[...]).astype(o_ref.dtype)


def _matmul_gate_kernel(h_ref, w_ref, b_ref, o_ref):
    o_ref[...] = _sigmoid_tanh(_dot(h_ref[...], w_ref[...]) + b_ref[...]).astype(o_ref.dtype)


def _matmul_call(h, w, bias, out_dtype, name, n_blocks, w_col=lambda j: j, tm=1024, tn=1024):
    m, k = h.shape
    n = n_blocks * tn
    tm = min(tm, m)
    in_specs = [pl.BlockSpec((tm, k), lambda i, j: (i, 0)),
                pl.BlockSpec((k, tn), lambda i, j: (0, w_col(j)))]
    args = [h, w]
    kern = _matmul_kernel
    if bias is not None:
        in_specs.append(pl.BlockSpec((1, tn), lambda i, j: (0, j)))
        args.append(bias)
        kern = _matmul_gate_kernel
    return pl.pallas_call(
        kern,
        grid=(m // tm, n // tn),
        in_specs=in_specs,
        out_specs=pl.BlockSpec((tm, tn), lambda i, j: (i, j)),
        out_shape=jax.ShapeDtypeStruct((m, n), out_dtype),
        compiler_params=pltpu.CompilerParams(
            dimension_semantics=("parallel", "parallel"), vmem_limit_bytes=VMEM_LIMIT),
        name=name,
    )(*args)


def _hgrn2_kernel(q_ref, f_ref, i_ref, g_ref, lbl_ref, ng_ref, o_ref, st_ref):
    c = HG_CHUNK
    n_chunks = q_ref.shape[0] // c
    n_heads = q_ref.shape[1] // HEAD

    @pl.when(pl.program_id(2) == 0)
    def _():
        st_ref[...] = jnp.zeros_like(st_ref)

    lbl = lbl_ref[...]
    e = jnp.exp(lbl - jnp.max(lbl, axis=0, keepdims=True))
    lb_all = e[0:1, :] / jnp.sum(e, axis=0, keepdims=True)

    row = lax.broadcasted_iota(jnp.int32, (c, c), 0)
    col = lax.broadcasted_iota(jnp.int32, (c, c), 1)
    causal = col <= row
    tril = jnp.where(causal, 1.0, 0.0).astype(BF16)
    tril2 = jnp.concatenate([tril, tril], axis=1)

    heads = [slice(hh * HEAD, (hh + 1) * HEAD) for hh in range(n_heads)]
    chunks = [slice(ci * c, (ci + 1) * c) for ci in range(n_chunks)]
    f_half = 0.5 * (1.0 - lb_all)
    f_mid = lb_all + f_half

    def decays(rs):
        f = f_mid + f_half * jnp.tanh(0.5 * f_ref[rs, :])
        lf2 = jnp.log2(f)
        hi = lf2.astype(BF16)
        lo = (lf2 - hi.astype(F32)).astype(BF16)
        return 1.0 - f, _dot(tril2, jnp.concatenate([hi, lo], axis=0))

    def operands(rs, cs, kk, bc):
        b_last = bc[c - 1:c, :]
        b_mid = bc[c // 2 - 1:c // 2, :]
        q = q_ref[rs, cs].astype(F32)
        q_dec = (q * jnp.exp2(bc)).astype(BF16)
        q_n = (q * jnp.exp2(bc - b_mid)).astype(BF16)
        k_mid = kk * jnp.exp2(b_mid - bc)
        k_end = (k_mid * jnp.exp2(b_last - b_mid)).astype(BF16)
        a = jnp.where(causal, _dot_nt(q_n, k_mid.astype(BF16)), 0.0).astype(BF16)
        v_t = i_ref[rs, cs].astype(F32).T.astype(BF16)
        return a, q_dec, k_end, v_t, jnp.exp2(b_last)

    def recur(rs, cs, st, a, q_dec, k_end, v_t, dec):
        o = _dot_nt(jnp.concatenate([a, q_dec], axis=1),
                    jnp.concatenate([v_t, st.astype(BF16)], axis=1))
        ms = jnp.mean(o * o, axis=-1, keepdims=True)
        g = g_ref[rs, cs].astype(F32)
        y = o * lax.rsqrt(ms + LN_EPS) * ng_ref[:, cs] * _silu(g)
        o_ref[rs, cs] = y.astype(o_ref.dtype)
        return st * dec + _dot(v_t, k_end)

    states = [st_ref[hh] for hh in range(n_heads)]
    prep, ops = {}, {}
    for step in range(n_chunks + 2):
        if step < n_chunks:
            prep[step] = decays(chunks[step])
        ci = step - 1
        if 0 <= ci < n_chunks:
            kk_all, bc_all = prep.pop(ci)
            for hh, cs in enumerate(heads):
                ops[ci, hh] = operands(chunks[ci], cs, kk_all[:, cs], bc_all[:, cs])
        ci = step - 2
        if 0 <= ci < n_chunks:
            for hh, cs in enumerate(heads):
                states[hh] = recur(chunks[ci], cs, states[hh], *ops.pop((ci, hh)))
    for hh in range(n_heads):
        st_ref[hh] = states[hh]


def _hgrn2_call(pbf, pf, lb_logits, norm_g, batch, seq, n_heads):
    m = batch * seq
    hb, tt = HG_HEADS_PER_STEP, min(HG_ROWS_PER_STEP, seq)
    w = hb * HEAD
    nt = seq // tt
    n_hg = n_heads // hb
    sec = n_heads * HEAD // w

    def rows(b, h, t):
        return b * nt + t

    return pl.pallas_call(
        _hgrn2_kernel,
        grid=(batch, n_hg, nt),
        in_specs=[pl.BlockSpec((tt, w), lambda b, h, t: (rows(b, h, t), h)),
                  pl.BlockSpec((tt, w), lambda b, h, t: (rows(b, h, t), h)),
                  pl.BlockSpec((tt, w), lambda b, h, t: (rows(b, h, t), sec + h)),
                  pl.BlockSpec((tt, w), lambda b, h, t: (rows(b, h, t), 2 * sec + h)),
                  pl.BlockSpec((2, w), lambda b, h, t: (0, h)),
                  pl.BlockSpec((1, w), lambda b, h, t: (0, h))],
        out_specs=pl.BlockSpec((tt, w), lambda b, h, t: (rows(b, h, t), h)),
        out_shape=jax.ShapeDtypeStruct((m, n_heads * HEAD), BF16),
        scratch_shapes=[pltpu.VMEM((hb, HEAD, HEAD), F32)],
        compiler_params=pltpu.CompilerParams(
            dimension_semantics=("parallel", "parallel", "arbitrary"),
            vmem_limit_bytes=VMEM_LIMIT),
        name="hgrn2",
    )(pbf, pf, pbf, pbf, lb_logits, norm_g)


def _softplus(z):
    return jnp.maximum(z, 0.0) + jnp.log(1.0 + jnp.exp2(jnp.abs(z) * -LOG2_E))


def _sb_scores(q, k, diagonal):
    blk = q.shape[0]
    z = _dot_nt(q, k)
    sp = _softplus(z)
    log_beta = z - sp
    strict = None
    if diagonal:
        qrow = lax.broadcasted_iota(jnp.int32, (blk, blk), 0)
        kcol = lax.broadcasted_iota(jnp.int32, (blk, blk), 1)
        strict = kcol < qrow
        sp = jnp.where(strict, sp, 0.0)
    return log_beta, sp.astype(BF16), jnp.sum(sp, axis=-1, keepdims=True), strict


def _sb_weights(log_beta, sp_b, later, s, strict):
    blk = log_beta.shape[0]
    log_a = log_beta - _dot(sp_b, later)
    if s is not None:
        log_a = log_a - jnp.concatenate([s] * (blk // HEAD), axis=1)
    a = jnp.exp(log_a)
    if strict is not None:
        a = jnp.where(strict, a, 0.0)
    return a.astype(BF16)


def _sb_sweep(units, later, acc, s):
    n = len(units)
    st1, st2 = {}, {}
    for t in range(n + 2):
        if t < n:
            _, q, k, _, diagonal = units[t]
            st1[t] = _sb_scores(q, k, diagonal)
        u = t - 1
        if 0 <= u < n:
            h = units[u][0]
            log_beta, sp_b, rowsum, strict = st1.pop(u)
            st2[u] = _sb_weights(log_beta, sp_b, later, s.get(h), strict)
            s[h] = s[h] + rowsum if h in s else jnp.broadcast_to(rowsum, (rowsum.shape[0], HEAD))
        u = t - 2
        if 0 <= u < n:
            h, v = units[u][0], units[u][3]
            part = _dot(st2.pop(u), v)
            acc[h] = acc[h] + part if h in acc else part


def _sb_kernel(k_col0, seq, q_ref, kd_ref, vd_ref, kp_ref, vp_ref, g_ref, later_ref, kv_hbm,
               o_ref, acc_ref, s_ref, kbuf, vbuf, sem):
    blk = SB_BLOCK
    b = pl.program_id(0)
    qi = pl.program_id(1)
    width = q_ref.shape[1]
    n_heads = width // HEAD
    heads = [slice(h * HEAD, (h + 1) * HEAD) for h in range(n_heads)]
    later = later_ref[...]
    has_prev = qi > 0

    units = []
    for h, cs in enumerate(heads):
        q = q_ref[:, cs]
        units.append((h, q, kd_ref[:, cs], vd_ref[:, cs], True))
        vp = jnp.where(has_prev, vp_ref[:, cs], jnp.zeros((blk, HEAD), BF16))
        units.append((h, q, kp_ref[:, cs], vp, False))
    acc, s = {}, {}
    _sb_sweep(units, later, acc, s)
    s_min = None
    for h in range(n_heads):
        acc_ref[h] = acc[h]
        s_ref[h] = s[h]
        s_min = s[h] if s_min is None else jnp.minimum(s_min, s[h])

    def cond(carry):
        kb, s_low = carry
        return jnp.logical_and(kb >= 0, s_low < SB_EXP_ZERO)

    def body(carry):
        kb, _ = carry
        row0 = pl.multiple_of(b * seq + kb * blk, blk)
        ck = pltpu.make_async_copy(
            kv_hbm.at[pl.ds(row0, blk), pl.ds(k_col0, width)], kbuf, sem.at[0])
        cv = pltpu.make_async_copy(
            kv_hbm.at[pl.ds(row0, blk), pl.ds(k_col0 + width, width)], vbuf, sem.at[1])
        ck.start()
        cv.start()
        ck.wait()
        cv.wait()
        units = [(h, q_ref[:, cs], kbuf[:, cs], vbuf[:, cs], False) for h, cs in enumerate(heads)]
        acc = {h: acc_ref[h] for h in range(n_heads)}
        s = {h: s_ref[h] for h in range(n_heads)}
        _sb_sweep(units, later, acc, s)
        s_low = None
        for h in range(n_heads):
            acc_ref[h] = acc[h]
            s_ref[h] = s[h]
            s_low = s[h] if s_low is None else jnp.minimum(s_low, s[h])
        return kb - 1, jnp.min(s_low)

    lax.while_loop(cond, body, (qi - 2, jnp.min(s_min)))

    for h, cs in enumerate(heads):
        g = g_ref[:, cs].astype(F32)
        o_ref[:, cs] = (acc_ref[h] * _silu(g)).astype(o_ref.dtype)


def _sb_call(pbf, later, batch, seq, n_heads, col0):
    m = batch * seq
    blk = SB_BLOCK
    nq = seq // blk
    width = n_heads * HEAD
    c0 = col0 // width

    def tile(sec):
        return pl.BlockSpec((blk, width), lambda b, i: (b * nq + i, c0 + sec))

    def prev(sec):
        return pl.BlockSpec((blk, width), lambda b, i: (b * nq + jnp.maximum(i - 1, 0), c0 + sec))

    return pl.pallas_call(
        functools.partial(_sb_kernel, col0 + width, seq),
        grid=(batch, nq),
        in_specs=[tile(0), tile(1), tile(2), prev(1), prev(2), tile(3),
                  pl.BlockSpec((blk, blk), lambda b, i: (0, 0)),
                  pl.BlockSpec(memory_space=pl.ANY)],
        out_specs=pl.BlockSpec((blk, width), lambda b, i: (b * nq + i, 0)),
        out_shape=jax.ShapeDtypeStruct((m, width), BF16),
        scratch_shapes=[pltpu.VMEM((n_heads, blk, HEAD), F32),
                        pltpu.VMEM((n_heads, blk, HEAD), F32),
                        pltpu.VMEM((blk, width), BF16),
                        pltpu.VMEM((blk, width), BF16),
                        pltpu.SemaphoreType.DMA((2,))],
        compiler_params=pltpu.CompilerParams(
            dimension_semantics=("parallel", "arbitrary"),
            vmem_limit_bytes=VMEM_LIMIT),
        name="stick_breaking",
    )(pbf, pbf, pbf, pbf, pbf, pbf, later, pbf)


def _tail_kernel(alpha, ya_ref, yb_ref, gt_ref, x_ref, p_ref, lig_ref, lib_ref,
                 wa_ref, wb_ref, wo_ref, wpg_ref, bpg_ref, wpe_ref, lg_ref, lb_ref, o_ref):
    d = o_ref.shape[1]
    ma = _dot(ya_ref[...], wa_ref[...])
    mb = _dot(yb_ref[...], wb_ref[...])
    mm = gt_ref[:, :d].astype(F32) * ma + gt_ref[:, d:].astype(F32) * mb
    h = _layer_norm(x_ref[...], lig_ref[...], lib_ref[...])
    r = alpha * h + _dot(mm.astype(BF16), wo_ref[...])
    pg = _sigmoid_tanh(_dot(r.astype(BF16), wpg_ref[...]) + bpg_ref[...])
    pe = _dot(p_ref[...].astype(BF16), wpe_ref[...])
    r = r + pg * pe
    o_ref[...] = _layer_norm(r, lg_ref[...], lb_ref[...])


def _tail_call(alpha, ya, yb, gates, x, p, lig, lib, wa, wb, wo, wpg, bpg, wpe, lg, lb, tm=256):
    m, d = x.shape
    tm = min(tm, m)

    def act(a):
        return pl.BlockSpec((tm, a.shape[1]), lambda i: (i, 0))

    def const(a):
        return pl.BlockSpec(a.shape, lambda i: (0, 0), pipeline_mode=pl.Buffered(1))

    return pl.pallas_call(
        functools.partial(_tail_kernel, alpha),
        grid=(m // tm,),
        in_specs=[act(ya), act(yb), act(gates), act(x), act(p), const(lig), const(lib),
                  const(wa), const(wb), const(wo), const(wpg), const(bpg), const(wpe),
                  const(lg), const(lb)],
        out_specs=pl.BlockSpec((tm, d), lambda i: (i, 0)),
        out_shape=jax.ShapeDtypeStruct((m, d), F32),
        compiler_params=pltpu.CompilerParams(
            dimension_semantics=("parallel",), vmem_limit_bytes=VMEM_LIMIT),
        name="tail",
    )(ya, yb, gates, x, p, lig, lib, wa, wb, wo, wpg, bpg, wpe, lg, lb)


def kernel(x, p, ln_in_g, ln_in_b, w_in, hg_lb_logits, hg_norm_g, w_merge, b_merge,
           w_br_hg, w_br_sb, w_out, w_pe, w_pg, b_pg, ln_g, ln_b):
    batch, seq, d = x.shape
    depth = w_in.shape[0]
    assert depth == 1, "kernel is written for a single layer"
    hg_w = w_br_hg.shape[1]
    sb_w = w_br_sb.shape[1]
    assert w_in.shape[2] == 4 * hg_w + 4 * sb_w
    hg_heads, sb_heads = hg_w // HEAD, sb_w // HEAD
    alpha = float((2 * depth) ** 0.25)
    m = batch * seq

    x2 = x.reshape(m, d)
    p2 = p[0].reshape(m, p.shape[-1])
    row = lambda a: a.reshape(1, -1)

    sq0 = 4 * hg_w
    col = jnp.arange(w_in.shape[2])
    col_scale = jnp.where((col >= sq0) & (col < sq0 + sb_w), HEAD ** -0.5, 1.0).astype(F32)
    w_all = (w_in[0] * col_scale).astype(BF16)
    tn = 2048
    hf0, n_hf = hg_w // tn, hg_w // tn
    n_bf = w_in.shape[2] // tn - n_hf

    assert n_hf == 1
    h, pf = _ln_proj_call(x2, row(ln_in_g), row(ln_in_b), w_all, hf0, tn)
    pbf = _matmul_call(h, w_all, None, BF16, "proj_bf16", n_bf,
                       lambda j: j + jnp.where(j >= hf0, n_hf, 0), tn=tn)
    gates = _matmul_call(h, w_merge[0].astype(BF16), row(b_merge[0]), BF16, "merge_gates",
                         w_merge.shape[2] // tn, tn=tn)

    ya = _hgrn2_call(pbf, pf, hg_lb_logits, row(hg_norm_g[0]), batch, seq, hg_heads)

    idx = jnp.arange(SB_BLOCK)
    later = (idx[:, None] > idx[None, :]).astype(BF16)
    yb = _sb_call(pbf, later, batch, seq, sb_heads, 3 * hg_w)

    out = _tail_call(alpha, ya, yb, gates, x2, p2, row(ln_in_g), row(ln_in_b),
                     w_br_hg[0].astype(BF16), w_br_sb[0].astype(BF16), w_out[0].astype(BF16),
                     w_pg[0].astype(BF16), row(b_pg[0]), w_pe[0].astype(BF16),
                     row(ln_g[0]), row(ln_b[0]))
    return out.reshape(batch, seq, d)
```

```python
import functools

import jax
import jax.numpy as jnp
from jax import lax
from jax.experimental import pallas as pl
from jax.experimental.pallas import tpu as pltpu

LN_EPS = 1e-5
LOG2_E = 1.4426950408889634
HEAD = 128
LN_SLAB = 256
HG_CHUNK = 128
HG_HEADS_PER_STEP = 4
HG_ROWS_PER_STEP = 1024
SB_BLOCK = 256
SB_EXP_ZERO = 104.0
VMEM_LIMIT = 52 * 1024 * 1024

F32 = jnp.float32
BF16 = jnp.bfloat16


def _dot(a, b):
    return jnp.dot(a, b, preferred_element_type=F32)


def _dot_nt(a, b):
    return lax.dot_general(a, b, (((1,), (1,)), ((), ())), preferred_element_type=F32)


def _layer_norm(x, g, b):
    mu = jnp.mean(x, axis=-1, keepdims=True)
    xc = x - mu
    var = jnp.mean(xc * xc, axis=-1, keepdims=True)
    return xc * lax.rsqrt(var + LN_EPS) * g + b


def _silu_of_half(half):
    return half * (jnp.tanh(half) + 1.0)


def _ln_proj_kernel(x_ref, g_ref, b_ref, w_ref, h_ref, o_ref):
    for r0 in range(0, x_ref.shape[0], LN_SLAB):
        rs = slice(r0, r0 + LN_SLAB)
        h = _layer_norm(x_ref[rs, :], g_ref[...], b_ref[...]).astype(h_ref.dtype)
        h_ref[rs, :] = h
        o_ref[rs, :] = _dot(h, w_ref[...])


def _ln_proj_call(x, g, b, w, w_block, tn, tm=512):
    m, d = x.shape
    tm = min(tm, m)
    return pl.pallas_call(
        _ln_proj_kernel,
        grid=(m // tm,),
        in_specs=[pl.BlockSpec((tm, d), lambda i: (i, 0)),
                  pl.BlockSpec((1, d), lambda i: (0, 0)),
                  pl.BlockSpec((1, d), lambda i: (0, 0)),
                  pl.BlockSpec((d, tn), lambda i: (0, w_block), pipeline_mode=pl.Buffered(1))],
        out_specs=[pl.BlockSpec((tm, d), lambda i: (i, 0)),
                   pl.BlockSpec((tm, tn), lambda i: (i, 0))],
        out_shape=[jax.ShapeDtypeStruct((m, d), BF16),
                   jax.ShapeDtypeStruct((m, tn), F32)],
        compiler_params=pltpu.CompilerParams(
            dimension_semantics=("parallel",), vmem_limit_bytes=VMEM_LIMIT),
        name="ln_proj_f32",
    )(x, g, b, w)


def _matmul_kernel(h_ref, w_ref, o_ref):
    o_ref[...] = _dot(h_ref[...], w_ref[...]).astype(o_ref.dtype)


def _matmul_gate_kernel(h_ref, w_ref, b_ref, o_ref):
    o_ref[...] = jnp.tanh(_dot(h_ref[...], w_ref[...]) + b_ref[...]).astype(o_ref.dtype)


def _matmul_call(h, w, bias, out_dtype, name, n_blocks, w_col=lambda j: j, tm=1024, tn=1024):
    m, k = h.shape
    n = n_blocks * tn
    tm = min(tm, m)
    in_specs = [pl.BlockSpec((tm, k), lambda i, j: (i, 0)),
                pl.BlockSpec((k, tn), lambda i, j: (0, w_col(j)))]
    args = [h, w]
    kern = _matmul_kernel
    if bias is not None:
        in_specs.append(pl.BlockSpec((1, tn), lambda i, j: (0, j)))
        args.append(bias)
        kern = _matmul_gate_kernel
    return pl.pallas_call(
        kern,
        grid=(m // tm, n // tn),
        in_specs=in_specs,
        out_specs=pl.BlockSpec((tm, tn), lambda i, j: (i, j)),
        out_shape=jax.ShapeDtypeStruct((m, n), out_dtype),
        compiler_params=pltpu.CompilerParams(
            dimension_semantics=("parallel", "parallel"), vmem_limit_bytes=VMEM_LIMIT),
        name=name,
    )(*args)


def _hgrn2_kernel(q_ref, f_ref, i_ref, g_ref, lbl_ref, ng_ref, o_ref, st_ref):
    c = HG_CHUNK
    n_chunks = q_ref.shape[0] // c
    n_heads = q_ref.shape[1] // HEAD

    @pl.when(pl.program_id(2) == 0)
    def _():
        st_ref[...] = jnp.zeros_like(st_ref)

    lbl = lbl_ref[...]
    e = jnp.exp(lbl - jnp.max(lbl, axis=0, keepdims=True))
    lb_all = e[0:1, :] / jnp.sum(e, axis=0, keepdims=True)

    row = lax.broadcasted_iota(jnp.int32, (c, c), 0)
    col = lax.broadcasted_iota(jnp.int32, (c, c), 1)
    causal = col <= row
    tril = jnp.where(causal, 1.0, 0.0).astype(BF16)
    tril2 = jnp.concatenate([tril, tril], axis=1)

    heads = [slice(hh * HEAD, (hh + 1) * HEAD) for hh in range(n_heads)]
    chunks = [slice(ci * c, (ci + 1) * c) for ci in range(n_chunks)]
    f_half = 0.5 * (1.0 - lb_all)
    f_mid = lb_all + f_half

    def decays(rs):
        f = f_mid + f_half * jnp.tanh(f_ref[rs, :])
        lf2 = jnp.log2(f)
        hi = lf2.astype(BF16)
        lo = (lf2 - hi.astype(F32)).astype(BF16)
        return 1.0 - f, _dot(tril2, jnp.concatenate([hi, lo], axis=0))

    def operands(rs, cs, kk, bc):
        b_last = bc[c - 1:c, :]
        b_mid = bc[c // 2 - 1:c // 2, :]
        q = q_ref[rs, cs].astype(F32)
        q_dec = (q * jnp.exp2(bc)).astype(BF16)
        q_n = (q * jnp.exp2(bc - b_mid)).astype(BF16)
        k_mid = kk * jnp.exp2(b_mid - bc)
        k_end = (k_mid * jnp.exp2(b_last - b_mid)).astype(BF16)
        a = jnp.where(causal, _dot_nt(q_n, k_mid.astype(BF16)), 0.0).astype(BF16)
        v_t = i_ref[rs, cs].astype(F32).T.astype(BF16)
        return a, q_dec, k_end, v_t, jnp.exp2(b_last)

    def recur(rs, cs, st, a, q_dec, k_end, v_t, dec):
        o = _dot_nt(jnp.concatenate([a, q_dec], axis=1),
                    jnp.concatenate([v_t, st.astype(BF16)], axis=1))
        ms = jnp.mean(o * o, axis=-1, keepdims=True)
        g = g_ref[rs, cs].astype(F32)
        y = o * lax.rsqrt(ms + LN_EPS) * ng_ref[:, cs] * _silu_of_half(g)
        o_ref[rs, cs] = y.astype(o_ref.dtype)
        return st * dec + _dot(v_t, k_end)

    states = [st_ref[hh] for hh in range(n_heads)]
    prep, ops = {}, {}
    for step in range(n_chunks + 2):
        if step < n_chunks:
            prep[step] = decays(chunks[step])
        ci = step - 1
        if 0 <= ci < n_chunks:
            kk_all, bc_all = prep.pop(ci)
            for hh, cs in enumerate(heads):
                ops[ci, hh] = operands(chunks[ci], cs, kk_all[:, cs], bc_all[:, cs])
        ci = step - 2
        if 0 <= ci < n_chunks:
            for hh, cs in enumerate(heads):
                states[hh] = recur(chunks[ci], cs, states[hh], *ops.pop((ci, hh)))
    for hh in range(n_heads):
        st_ref[hh] = states[hh]


def _hgrn2_call(pbf, pf, lb_logits, norm_g, batch, seq, n_heads):
    m = batch * seq
    hb, tt = HG_HEADS_PER_STEP, min(HG_ROWS_PER_STEP, seq)
    w = hb * HEAD
    nt = seq // tt
    n_hg = n_heads // hb
    sec = n_heads * HEAD // w

    def rows(b, h, t):
        return b * nt + t

    return pl.pallas_call(
        _hgrn2_kernel,
        grid=(batch, n_hg, nt),
        in_specs=[pl.BlockSpec((tt, w), lambda b, h, t: (rows(b, h, t), h)),
                  pl.BlockSpec((tt, w), lambda b, h, t: (rows(b, h, t), h)),
                  pl.BlockSpec((tt, w), lambda b, h, t: (rows(b, h, t), sec + h)),
                  pl.BlockSpec((tt, w), lambda b, h, t: (rows(b, h, t), 2 * sec + h)),
                  pl.BlockSpec((2, w), lambda b, h, t: (0, h)),
                  pl.BlockSpec((1, w), lambda b, h, t: (0, h))],
        out_specs=pl.BlockSpec((tt, w), lambda b, h, t: (rows(b, h, t), h)),
        out_shape=jax.ShapeDtypeStruct((m, n_heads * HEAD), BF16),
        scratch_shapes=[pltpu.VMEM((hb, HEAD, HEAD), F32)],
        compiler_params=pltpu.CompilerParams(
            dimension_semantics=("parallel", "parallel", "arbitrary"),
            vmem_limit_bytes=VMEM_LIMIT),
        name="hgrn2",
    )(pbf, pf, pbf, pbf, lb_logits, norm_g)


def _softplus(z):
    return jnp.maximum(z, 0.0) + jnp.log(1.0 + jnp.exp2(jnp.abs(z) * -LOG2_E))


def _sb_scores(q, k, diagonal):
    blk = q.shape[0]
    z = _dot_nt(q, k)
    sp = _softplus(z)
    log_beta = z - sp
    strict = None
    if diagonal:
        qrow = lax.broadcasted_iota(jnp.int32, (blk, blk), 0)
        kcol = lax.broadcasted_iota(jnp.int32, (blk, blk), 1)
        strict = kcol < qrow
        sp = jnp.where(strict, sp, 0.0)
    return log_beta, sp.astype(BF16), jnp.sum(sp, axis=-1, keepdims=True), strict


def _sb_weights(log_beta, sp_b, later, s, strict):
    blk = log_beta.shape[0]
    log_a = log_beta - _dot(sp_b, later)
    if s is not None:
        log_a = log_a - jnp.concatenate([s] * (blk // HEAD), axis=1)
    a = jnp.exp(log_a)
    if strict is not None:
        a = jnp.where(strict, a, 0.0)
    return a.astype(BF16)


def _sb_sweep(units, later, acc, s):
    n = len(units)
    st1, st2 = {}, {}
    for t in range(n + 2):
        if t < n:
            _, q, k, _, diagonal = units[t]
            st1[t] = _sb_scores(q, k, diagonal)
        u = t - 1
        if 0 <= u < n:
            h = units[u][0]
            log_beta, sp_b, rowsum, strict = st1.pop(u)
            st2[u] = _sb_weights(log_beta, sp_b, later, s.get(h), strict)
            s[h] = s[h] + rowsum if h in s else jnp.broadcast_to(rowsum, (rowsum.shape[0], HEAD))
        u = t - 2
        if 0 <= u < n:
            h, v = units[u][0], units[u][3]
            part = _dot(st2.pop(u), v)
            acc[h] = acc[h] + part if h in acc else part


def _sb_kernel(k_col0, seq, q_ref, kd_ref, vd_ref, kp_ref, vp_ref, g_ref, later_ref, kv_hbm,
               o_ref, acc_ref, s_ref, kbuf, vbuf, sem):
    blk = SB_BLOCK
    b = pl.program_id(0)
    qi = pl.program_id(1)
    width = q_ref.shape[1]
    n_heads = width // HEAD
    heads = [slice(h * HEAD, (h + 1) * HEAD) for h in range(n_heads)]
    later = later_ref[...]
    has_prev = qi > 0

    units = []
    for h, cs in enumerate(heads):
        q = q_ref[:, cs]
        units.append((h, q, kd_ref[:, cs], vd_ref[:, cs], True))
        vp = jnp.where(has_prev, vp_ref[:, cs], jnp.zeros((blk, HEAD), BF16))
        units.append((h, q, kp_ref[:, cs], vp, False))
    acc, s = {}, {}
    _sb_sweep(units, later, acc, s)
    s_min = None
    for h in range(n_heads):
        acc_ref[h] = acc[h]
        s_ref[h] = s[h]
        s_min = s[h] if s_min is None else jnp.minimum(s_min, s[h])

    def cond(carry):
        kb, s_low = carry
        return jnp.logical_and(kb >= 0, s_low < SB_EXP_ZERO)

    def body(carry):
        kb, _ = carry
        row0 = pl.multiple_of(b * seq + kb * blk, blk)
        ck = pltpu.make_async_copy(
            kv_hbm.at[pl.ds(row0, blk), pl.ds(k_col0, width)], kbuf, sem.at[0])
        cv = pltpu.make_async_copy(
            kv_hbm.at[pl.ds(row0, blk), pl.ds(k_col0 + width, width)], vbuf, sem.at[1])
        ck.start()
        cv.start()
        ck.wait()
        cv.wait()
        units = [(h, q_ref[:, cs], kbuf[:, cs], vbuf[:, cs], False) for h, cs in enumerate(heads)]
        acc = {h: acc_ref[h] for h in range(n_heads)}
        s = {h: s_ref[h] for h in range(n_heads)}
        _sb_sweep(units, later, acc, s)
        s_low = None
        for h in range(n_heads):
            acc_ref[h] = acc[h]
            s_ref[h] = s[h]
            s_low = s[h] if s_low is None else jnp.minimum(s_low, s[h])
        return kb - 1, jnp.min(s_low)

    lax.while_loop(cond, body, (qi - 2, jnp.min(s_min)))

    for h, cs in enumerate(heads):
        g = g_ref[:, cs].astype(F32)
        o_ref[:, cs] = (acc_ref[h] * _silu_of_half(g)).astype(o_ref.dtype)


def _sb_call(pbf, later, batch, seq, n_heads, col0):
    m = batch * seq
    blk = SB_BLOCK
    nq = seq // blk
    width = n_heads * HEAD
    c0 = col0 // width

    def tile(sec):
        return pl.BlockSpec((blk, width), lambda b, i: (b * nq + i, c0 + sec))

    def prev(sec):
        return pl.BlockSpec((blk, width), lambda b, i: (b * nq + jnp.maximum(i - 1, 0), c0 + sec))

    return pl.pallas_call(
        functools.partial(_sb_kernel, col0 + width, seq),
        grid=(batch, nq),
        in_specs=[tile(0), tile(1), tile(2), prev(1), prev(2), tile(3),
                  pl.BlockSpec((blk, blk), lambda b, i: (0, 0)),
                  pl.BlockSpec(memory_space=pl.ANY)],
        out_specs=pl.BlockSpec((blk, width), lambda b, i: (b * nq + i, 0)),
        out_shape=jax.ShapeDtypeStruct((m, width), BF16),
        scratch_shapes=[pltpu.VMEM((n_heads, blk, HEAD), F32),
                        pltpu.VMEM((n_heads, blk, HEAD), F32),
                        pltpu.VMEM((blk, width), BF16),
                        pltpu.VMEM((blk, width), BF16),
                        pltpu.SemaphoreType.DMA((2,))],
        compiler_params=pltpu.CompilerParams(
            dimension_semantics=("parallel", "arbitrary"),
            vmem_limit_bytes=VMEM_LIMIT),
        name="stick_breaking",
    )(pbf, pbf, pbf, pbf, pbf, pbf, later, pbf)


def _tail_kernel(alpha, ya_ref, yb_ref, gt_ref, x_ref, p_ref, lig_ref, lib_ref,
                 wa_ref, wb_ref, wo_ref, wpg_ref, bpg_ref, wpe_ref, lg_ref, lb_ref, o_ref):
    d = o_ref.shape[1]
    ma = _dot(ya_ref[...], wa_ref[...])
    mb = _dot(yb_ref[...], wb_ref[...])
    mm = (gt_ref[:, :d].astype(F32) * ma + ma) + (gt_ref[:, d:].astype(F32) * mb + mb)
    h = _layer_norm(x_ref[...], lig_ref[...], lib_ref[...])
    r = alpha * h + _dot(mm.astype(BF16), wo_ref[...])
    tg = jnp.tanh(_dot(r.astype(BF16), wpg_ref[...]) + bpg_ref[...])
    pe = _dot(p_ref[...].astype(BF16), wpe_ref[...])
    r = r + (tg * pe + pe)
    o_ref[...] = _layer_norm(r, lg_ref[...], lb_ref[...])


def _tail_call(alpha, ya, yb, gates, x, p, lig, lib, wa, wb, wo, wpg, bpg, wpe, lg, lb, tm=256):
    m, d = x.shape
    tm = min(tm, m)

    def act(a):
        return pl.BlockSpec((tm, a.shape[1]), lambda i: (i, 0))

    def const(a):
        return pl.BlockSpec(a.shape, lambda i: (0, 0), pipeline_mode=pl.Buffered(1))

    return pl.pallas_call(
        functools.partial(_tail_kernel, alpha),
        grid=(m // tm,),
        in_specs=[act(ya), act(yb), act(gates), act(x), act(p), const(lig), const(lib),
                  const(wa), const(wb), const(wo), const(wpg), const(bpg), const(wpe),
                  const(lg), const(lb)],
        out_specs=pl.BlockSpec((tm, d), lambda i: (i, 0)),
        out_shape=jax.ShapeDtypeStruct((m, d), F32),
        compiler_params=pltpu.CompilerParams(
            dimension_semantics=("parallel",), vmem_limit_bytes=VMEM_LIMIT),
        name="tail",
    )(ya, yb, gates, x, p, lig, lib, wa, wb, wo, wpg, bpg, wpe, lg, lb)


def kernel(x, p, ln_in_g, ln_in_b, w_in, hg_lb_logits, hg_norm_g, w_merge, b_merge,
           w_br_hg, w_br_sb, w_out, w_pe, w_pg, b_pg, ln_g, ln_b):
    batch, seq, d = x.shape
    depth = w_in.shape[0]
    assert depth == 1, "kernel is written for a single layer"
    hg_w = w_br_hg.shape[1]
    sb_w = w_br_sb.shape[1]
    assert w_in.shape[2] == 4 * hg_w + 4 * sb_w
    hg_heads, sb_heads = hg_w // HEAD, sb_w // HEAD
    alpha = float((2 * depth) ** 0.25)
    m = batch * seq

    x2 = x.reshape(m, d)
    p2 = p[0].reshape(m, p.shape[-1])
    row = lambda a: a.reshape(1, -1)

    sec_scale = [1.0, 0.5, 1.0, 0.5] + [HEAD ** -0.5, 1.0, 1.0, 0.5]
    col_scale = jnp.concatenate(
        [jnp.full((hg_w,), sc, F32) for sc in sec_scale[:4]]
        + [jnp.full((sb_w,), sc, F32) for sc in sec_scale[4:]])
    w_all = (w_in[0] * col_scale).astype(BF16)
    tn = 2048
    hf0, n_hf = hg_w // tn, hg_w // tn
    n_bf = w_in.shape[2] // tn - n_hf

    assert n_hf == 1
    h, pf = _ln_proj_call(x2, row(ln_in_g), row(ln_in_b), w_all, hf0, tn)
    pbf = _matmul_call(h, w_all, None, BF16, "proj_bf16", n_bf,
                       lambda j: j + jnp.where(j >= hf0, n_hf, 0), tn=tn)
    gates = _matmul_call(h, (0.5 * w_merge[0]).astype(BF16), row(0.5 * b_merge[0]), BF16, "merge_gates",
                         w_merge.shape[2] // tn, tn=tn)

    ya = _hgrn2_call(pbf, pf, hg_lb_logits, row(hg_norm_g[0]), batch, seq, hg_heads)

    idx = jnp.arange(SB_BLOCK)
    later = (idx[:, None] > idx[None, :]).astype(BF16)
    yb = _sb_call(pbf, later, batch, seq, sb_heads, 3 * hg_w)

    out = _tail_call(alpha, ya, yb, gates, x2, p2, row(ln_in_g), row(ln_in_b),
                     w_br_hg[0].astype(BF16), w_br_sb[0].astype(BF16), (0.5 * w_out[0]).astype(BF16),
                     (0.5 * w_pg[0]).astype(BF16), row(0.5 * b_pg[0]), (0.5 * w_pe[0]).astype(BF16),
                     row(ln_g[0]), row(ln_b[0]))
    return out.reshape(batch, seq, d)
```

```python
import functools

import jax
import jax.numpy as jnp
from jax import lax
from jax.experimental import pallas as pl
from jax.experimental.pallas import tpu as pltpu

LN_EPS = 1e-5
LOG2_E = 1.4426950408889634
HEAD = 128
LN_SLAB = 256
HG_CHUNK = 128
HG_HEADS_PER_STEP = 4
HG_ROWS_PER_STEP = 1024
SB_BLOCK = 256
SB_EXP_ZERO = 104.0
SB_GATE_COLS = 2048
SB_GATE_CHUNK = 256
VMEM_LIMIT = 52 * 1024 * 1024

F32 = jnp.float32
BF16 = jnp.bfloat16


def _dot(a, b):
    return jnp.dot(a, b, preferred_element_type=F32)


def _dot_nt(a, b):
    return lax.dot_general(a, b, (((1,), (1,)), ((), ())), preferred_element_type=F32)


def _layer_norm(x, g, b):
    mu = jnp.mean(x, axis=-1, keepdims=True)
    xc = x - mu
    var = jnp.mean(xc * xc, axis=-1, keepdims=True)
    return xc * lax.rsqrt(var + LN_EPS) * g + b


def _silu_of_half(half):
    return half * (jnp.tanh(half) + 1.0)


def _ln_proj_kernel(x_ref, g_ref, b_ref, w_ref, h_ref, o_ref):
    for r0 in range(0, x_ref.shape[0], LN_SLAB):
        rs = slice(r0, r0 + LN_SLAB)
        h = _layer_norm(x_ref[rs, :], g_ref[...], b_ref[...]).astype(h_ref.dtype)
        h_ref[rs, :] = h
        o_ref[rs, :] = _dot(h, w_ref[...])


def _ln_proj_call(x, g, b, w, w_block, tn, tm=512):
    m, d = x.shape
    tm = min(tm, m)
    return pl.pallas_call(
        _ln_proj_kernel,
        grid=(m // tm,),
        in_specs=[pl.BlockSpec((tm, d), lambda i: (i, 0)),
                  pl.BlockSpec((1, d), lambda i: (0, 0)),
                  pl.BlockSpec((1, d), lambda i: (0, 0)),
                  pl.BlockSpec((d, tn), lambda i: (0, w_block), pipeline_mode=pl.Buffered(1))],
        out_specs=[pl.BlockSpec((tm, d), lambda i: (i, 0)),
                   pl.BlockSpec((tm, tn), lambda i: (i, 0))],
        out_shape=[jax.ShapeDtypeStruct((m, d), BF16),
                   jax.ShapeDtypeStruct((m, tn), F32)],
        compiler_params=pltpu.CompilerParams(
            dimension_semantics=("parallel",), vmem_limit_bytes=VMEM_LIMIT),
        name="ln_proj_f32",
    )(x, g, b, w)


def _matmul_kernel(h_ref, w_ref, o_ref):
    o_ref[...] = _dot(h_ref[...], w_ref[...]).astype(o_ref.dtype)


def _matmul_gate_kernel(h_ref, w_ref, b_ref, o_ref):
    o_ref[...] = jnp.tanh(_dot(h_ref[...], w_ref[...]) + b_ref[...]).astype(o_ref.dtype)


def _matmul_call(h, w, bias, out_dtype, name, n_blocks, w_col=lambda j: j, tm=1024, tn=1024):
    m, k = h.shape
    n = n_blocks * tn
    tm = min(tm, m)
    in_specs = [pl.BlockSpec((tm, k), lambda i, j: (i, 0)),
                pl.BlockSpec((k, tn), lambda i, j: (0, w_col(j)))]
    args = [h, w]
    kern = _matmul_kernel
    if bias is not None:
        in_specs.append(pl.BlockSpec((1, tn), lambda i, j: (0, j)))
        args.append(bias)
        kern = _matmul_gate_kernel
    return pl.pallas_call(
        kern,
        grid=(m // tm, n // tn),
        in_specs=in_specs,
        out_specs=pl.BlockSpec((tm, tn), lambda i, j: (i, j)),
        out_shape=jax.ShapeDtypeStruct((m, n), out_dtype),
        compiler_params=pltpu.CompilerParams(
            dimension_semantics=("parallel", "parallel"), vmem_limit_bytes=VMEM_LIMIT),
        name=name,
    )(*args)


def _hgrn2_kernel(q_ref, f_ref, i_ref, g_ref, lbl_ref, ng_ref, o_ref, st_ref):
    c = HG_CHUNK
    n_chunks = q_ref.shape[0] // c
    n_heads = q_ref.shape[1] // HEAD

    @pl.when(pl.program_id(2) == 0)
    def _():
        st_ref[...] = jnp.zeros_like(st_ref)

    lbl = lbl_ref[...]
    e = jnp.exp(lbl - jnp.max(lbl, axis=0, keepdims=True))
    lb_all = e[0:1, :] / jnp.sum(e, axis=0, keepdims=True)

    row = lax.broadcasted_iota(jnp.int32, (c, c), 0)
    col = lax.broadcasted_iota(jnp.int32, (c, c), 1)
    causal = col <= row
    tril = jnp.where(causal, 1.0, 0.0).astype(BF16)
    tril2 = jnp.concatenate([tril, tril], axis=1)

    heads = [slice(hh * HEAD, (hh + 1) * HEAD) for hh in range(n_heads)]
    chunks = [slice(ci * c, (ci + 1) * c) for ci in range(n_chunks)]
    f_half = 0.5 * (1.0 - lb_all)
    f_mid = lb_all + f_half

    def decays(rs):
        f = f_mid + f_half * jnp.tanh(f_ref[rs, :])
        lf2 = jnp.log2(f)
        hi = lf2.astype(BF16)
        lo = (lf2 - hi.astype(F32)).astype(BF16)
        return 1.0 - f, _dot(tril2, jnp.concatenate([hi, lo], axis=0))

    def operands(rs, cs, kk, bc):
        b_last = bc[c - 1:c, :]
        b_mid = bc[c // 2 - 1:c // 2, :]
        q = q_ref[rs, cs].astype(F32)
        q_dec = (q * jnp.exp2(bc)).astype(BF16)
        q_n = (q * jnp.exp2(bc - b_mid)).astype(BF16)
        k_mid = kk * jnp.exp2(b_mid - bc)
        k_end = (k_mid * jnp.exp2(b_last - b_mid)).astype(BF16)
        a = jnp.where(causal, _dot_nt(q_n, k_mid.astype(BF16)), 0.0).astype(BF16)
        v_t = i_ref[rs, cs].astype(F32).T.astype(BF16)
        return a, q_dec, k_end, v_t, jnp.exp2(b_last)

    def recur(rs, cs, st, a, q_dec, k_end, v_t, dec):
        o = _dot_nt(jnp.concatenate([a, q_dec], axis=1),
                    jnp.concatenate([v_t, st.astype(BF16)], axis=1))
        ms = jnp.mean(o * o, axis=-1, keepdims=True)
        g = g_ref[rs, cs].astype(F32)
        y = o * lax.rsqrt(ms + LN_EPS) * ng_ref[:, cs] * _silu_of_half(g)
        o_ref[rs, cs] = y.astype(o_ref.dtype)
        return st * dec + _dot(v_t, k_end)

    states = [st_ref[hh] for hh in range(n_heads)]
    prep, ops = {}, {}
    for step in range(n_chunks + 2):
        if step < n_chunks:
            prep[step] = decays(chunks[step])
        ci = step - 1
        if 0 <= ci < n_chunks:
            kk_all, bc_all = prep.pop(ci)
            for hh, cs in enumerate(heads):
                ops[ci, hh] = operands(chunks[ci], cs, kk_all[:, cs], bc_all[:, cs])
        ci = step - 2
        if 0 <= ci < n_chunks:
            for hh, cs in enumerate(heads):
                states[hh] = recur(chunks[ci], cs, states[hh], *ops.pop((ci, hh)))
    for hh in range(n_heads):
        st_ref[hh] = states[hh]


def _hgrn2_call(pbf, pf, lb_logits, norm_g, batch, seq, n_heads):
    m = batch * seq
    hb, tt = HG_HEADS_PER_STEP, min(HG_ROWS_PER_STEP, seq)
    w = hb * HEAD
    nt = seq // tt
    n_hg = n_heads // hb
    sec = n_heads * HEAD // w

    def rows(b, h, t):
        return b * nt + t

    return pl.pallas_call(
        _hgrn2_kernel,
        grid=(batch, n_hg, nt),
        in_specs=[pl.BlockSpec((tt, w), lambda b, h, t: (rows(b, h, t), h)),
                  pl.BlockSpec((tt, w), lambda b, h, t: (rows(b, h, t), h)),
                  pl.BlockSpec((tt, w), lambda b, h, t: (rows(b, h, t), sec + h)),
                  pl.BlockSpec((tt, w), lambda b, h, t: (rows(b, h, t), 2 * sec + h)),
                  pl.BlockSpec((2, w), lambda b, h, t: (0, h)),
                  pl.BlockSpec((1, w), lambda b, h, t: (0, h))],
        out_specs=pl.BlockSpec((tt, w), lambda b, h, t: (rows(b, h, t), h)),
        out_shape=jax.ShapeDtypeStruct((m, n_heads * HEAD), BF16),
        scratch_shapes=[pltpu.VMEM((hb, HEAD, HEAD), F32)],
        compiler_params=pltpu.CompilerParams(
            dimension_semantics=("parallel", "parallel", "arbitrary"),
            vmem_limit_bytes=VMEM_LIMIT),
        name="hgrn2",
    )(pbf, pf, pbf, pbf, lb_logits, norm_g)


def _softplus(z):
    return jnp.maximum(z, 0.0) + jnp.log(1.0 + jnp.exp2(jnp.abs(z) * -LOG2_E))


def _sb_scores(q, k, diagonal):
    blk = q.shape[0]
    z = _dot_nt(q, k)
    sp = _softplus(z)
    log_beta = z - sp
    strict = None
    if diagonal:
        qrow = lax.broadcasted_iota(jnp.int32, (blk, blk), 0)
        kcol = lax.broadcasted_iota(jnp.int32, (blk, blk), 1)
        strict = kcol < qrow
        sp = jnp.where(strict, sp, 0.0)
    return log_beta, sp.astype(BF16), jnp.sum(sp, axis=-1, keepdims=True), strict


def _sb_weights(log_beta, sp_b, later, s, strict):
    blk = log_beta.shape[0]
    log_a = log_beta - _dot(sp_b, later)
    if s is not None:
        log_a = log_a - jnp.concatenate([s] * (blk // HEAD), axis=1)
    a = jnp.exp(log_a)
    if strict is not None:
        a = jnp.where(strict, a, 0.0)
    return a.astype(BF16)


def _sb_sweep(units, later, acc, s, side_work=()):
    n = len(units)
    st1, st2 = {}, {}
    side_work = list(side_work)
    stride = max(1, (n + 2) // max(1, len(side_work)))
    for t in range(n + 2):
        if side_work and t % stride == 0:
            side_work.pop(0)()
        if t < n:
            _, q, k, _, diagonal = units[t]
            st1[t] = _sb_scores(q, k, diagonal)
        u = t - 1
        if 0 <= u < n:
            h = units[u][0]
            log_beta, sp_b, rowsum, strict = st1.pop(u)
            st2[u] = _sb_weights(log_beta, sp_b, later, s.get(h), strict)
            s[h] = s[h] + rowsum if h in s else jnp.broadcast_to(rowsum, (rowsum.shape[0], HEAD))
        u = t - 2
        if 0 <= u < n:
            h, v = units[u][0], units[u][3]
            part = _dot(st2.pop(u), v)
            acc[h] = acc[h] + part if h in acc else part
    for work in side_work:
        work()


def _sb_kernel(k_col0, seq, q_ref, kd_ref, vd_ref, kp_ref, vp_ref, g_ref, later_ref, kv_hbm,
               h_ref, wm_ref, bm_ref, o_ref, gt_ref, acc_ref, s_ref, kbuf, vbuf, sem):
    blk = SB_BLOCK
    b = pl.program_id(0)
    qi = pl.program_id(1)
    width = q_ref.shape[1]
    n_heads = width // HEAD
    heads = [slice(h * HEAD, (h + 1) * HEAD) for h in range(n_heads)]
    later = later_ref[...]
    has_prev = qi > 0

    units = []
    for h, cs in enumerate(heads):
        q = q_ref[:, cs]
        units.append((h, q, kd_ref[:, cs], vd_ref[:, cs], True))
        vp = jnp.where(has_prev, vp_ref[:, cs], jnp.zeros((blk, HEAD), BF16))
        units.append((h, q, kp_ref[:, cs], vp, False))
    def gate_chunk(c0):
        def work():
            cs = slice(c0, c0 + SB_GATE_CHUNK)
            gate = _dot(h_ref[...], wm_ref[:, cs]) + bm_ref[:, cs]
            gt_ref[:, cs] = jnp.tanh(gate).astype(gt_ref.dtype)
        return work

    acc, s = {}, {}
    _sb_sweep(units, later, acc, s,
              [gate_chunk(c0) for c0 in range(0, gt_ref.shape[1], SB_GATE_CHUNK)])
    s_min = None
    for h in range(n_heads):
        acc_ref[h] = acc[h]
        s_ref[h] = s[h]
        s_min = s[h] if s_min is None else jnp.minimum(s_min, s[h])

    def cond(carry):
        kb, s_low = carry
        return jnp.logical_and(kb >= 0, s_low < SB_EXP_ZERO)

    def body(carry):
        kb, _ = carry
        row0 = pl.multiple_of(b * seq + kb * blk, blk)
        ck = pltpu.make_async_copy(
            kv_hbm.at[pl.ds(row0, blk), pl.ds(k_col0, width)], kbuf, sem.at[0])
        cv = pltpu.make_async_copy(
            kv_hbm.at[pl.ds(row0, blk), pl.ds(k_col0 + width, width)], vbuf, sem.at[1])
        ck.start()
        cv.start()
        ck.wait()
        cv.wait()
        units = [(h, q_ref[:, cs], kbuf[:, cs], vbuf[:, cs], False) for h, cs in enumerate(heads)]
        acc = {h: acc_ref[h] for h in range(n_heads)}
        s = {h: s_ref[h] for h in range(n_heads)}
        _sb_sweep(units, later, acc, s)
        s_low = None
        for h in range(n_heads):
            acc_ref[h] = acc[h]
            s_ref[h] = s[h]
            s_low = s[h] if s_low is None else jnp.minimum(s_low, s[h])
        return kb - 1, jnp.min(s_low)

    lax.while_loop(cond, body, (qi - 2, jnp.min(s_min)))

    for h, cs in enumerate(heads):
        g = g_ref[:, cs].astype(F32)
        o_ref[:, cs] = (acc_ref[h] * _silu_of_half(g)).astype(o_ref.dtype)


def _sb_gates_call(pbf, later, h, w_merge_half, b_merge_half, batch, seq, n_heads, col0):
    m = batch * seq
    blk = SB_BLOCK
    nq = seq // blk
    width = n_heads * HEAD
    c0 = col0 // width
    d, gw = w_merge_half.shape
    gn = SB_GATE_COLS
    nj = gw // gn
    gm = m * nj // (batch * nq)
    assert gm * batch * nq == m * nj and gm % 16 == 0 and gn % SB_GATE_CHUNK == 0

    def tile(sec):
        return pl.BlockSpec((blk, width), lambda b, i: (b * nq + i, c0 + sec))

    def prev(sec):
        return pl.BlockSpec((blk, width), lambda b, i: (b * nq + jnp.maximum(i - 1, 0), c0 + sec))

    def step(b, i):
        return b * nq + i

    return pl.pallas_call(
        functools.partial(_sb_kernel, col0 + width, seq),
        grid=(batch, nq),
        in_specs=[tile(0), tile(1), tile(2), prev(1), prev(2), tile(3),
                  pl.BlockSpec((blk, blk), lambda b, i: (0, 0)),
                  pl.BlockSpec(memory_space=pl.ANY),
                  pl.BlockSpec((gm, d), lambda b, i: (step(b, i) // nj, 0)),
                  pl.BlockSpec((d, gn), lambda b, i: (0, step(b, i) % nj)),
                  pl.BlockSpec((1, gn), lambda b, i: (0, step(b, i) % nj))],
        out_specs=[pl.BlockSpec((blk, width), lambda b, i: (b * nq + i, 0)),
                   pl.BlockSpec((gm, gn), lambda b, i: (step(b, i) // nj, step(b, i) % nj))],
        out_shape=[jax.ShapeDtypeStruct((m, width), BF16),
                   jax.ShapeDtypeStruct((m, gw), BF16)],
        scratch_shapes=[pltpu.VMEM((n_heads, blk, HEAD), F32),
                        pltpu.VMEM((n_heads, blk, HEAD), F32),
                        pltpu.VMEM((blk, width), BF16),
                        pltpu.VMEM((blk, width), BF16),
                        pltpu.SemaphoreType.DMA((2,))],
        compiler_params=pltpu.CompilerParams(
            dimension_semantics=("parallel", "arbitrary"),
            vmem_limit_bytes=VMEM_LIMIT),
        name="stick_breaking_gates",
    )(pbf, pbf, pbf, pbf, pbf, pbf, later, pbf, h, w_merge_half, b_merge_half)


def _tail_kernel(alpha, ya_ref, yb_ref, gt_ref, x_ref, p_ref, lig_ref, lib_ref,
                 wa_ref, wb_ref, wo_ref, wpg_ref, bpg_ref, wpe_ref, lg_ref, lb_ref, o_ref):
    d = o_ref.shape[1]
    ma = _dot(ya_ref[...], wa_ref[...])
    mb = _dot(yb_ref[...], wb_ref[...])
    mm = (gt_ref[:, :d].astype(F32) * ma + ma) + (gt_ref[:, d:].astype(F32) * mb + mb)
    h = _layer_norm(x_ref[...], lig_ref[...], lib_ref[...])
    r = alpha * h + _dot(mm.astype(BF16), wo_ref[...])
    tg = jnp.tanh(_dot(r.astype(BF16), wpg_ref[...]) + bpg_ref[...])
    pe = _dot(p_ref[...].astype(BF16), wpe_ref[...])
    r = r + (tg * pe + pe)
    o_ref[...] = _layer_norm(r, lg_ref[...], lb_ref[...])


def _tail_call(alpha, ya, yb, gates, x, p, lig, lib, wa, wb, wo, wpg, bpg, wpe, lg, lb, tm=256):
    m, d = x.shape
    tm = min(tm, m)

    def act(a):
        return pl.BlockSpec((tm, a.shape[1]), lambda i: (i, 0))

    def const(a):
        return pl.BlockSpec(a.shape, lambda i: (0, 0), pipeline_mode=pl.Buffered(1))

    return pl.pallas_call(
        functools.partial(_tail_kernel, alpha),
        grid=(m // tm,),
        in_specs=[act(ya), act(yb), act(gates), act(x), act(p), const(lig), const(lib),
                  const(wa), const(wb), const(wo), const(wpg), const(bpg), const(wpe),
                  const(lg), const(lb)],
        out_specs=pl.BlockSpec((tm, d), lambda i: (i, 0)),
        out_shape=jax.ShapeDtypeStruct((m, d), F32),
        compiler_params=pltpu.CompilerParams(
            dimension_semantics=("parallel",), vmem_limit_bytes=VMEM_LIMIT),
        name="tail",
    )(ya, yb, gates, x, p, lig, lib, wa, wb, wo, wpg, bpg, wpe, lg, lb)


def kernel(x, p, ln_in_g, ln_in_b, w_in, hg_lb_logits, hg_norm_g, w_merge, b_merge,
           w_br_hg, w_br_sb, w_out, w_pe, w_pg, b_pg, ln_g, ln_b):
    batch, seq, d = x.shape
    depth = w_in.shape[0]
    assert depth == 1, "kernel is written for a single layer"
    hg_w = w_br_hg.shape[1]
    sb_w = w_br_sb.shape[1]
    assert w_in.shape[2] == 4 * hg_w + 4 * sb_w
    hg_heads, sb_heads = hg_w // HEAD, sb_w // HEAD
    alpha = float((2 * depth) ** 0.25)
    m = batch * seq

    x2 = x.reshape(m, d)
    p2 = p[0].reshape(m, p.shape[-1])
    row = lambda a: a.reshape(1, -1)

    sec_scale = [1.0, 0.5, 1.0, 0.5] + [HEAD ** -0.5, 1.0, 1.0, 0.5]
    col_scale = jnp.concatenate(
        [jnp.full((hg_w,), sc, F32) for sc in sec_scale[:4]]
        + [jnp.full((sb_w,), sc, F32) for sc in sec_scale[4:]])
    w_all = (w_in[0] * col_scale).astype(BF16)
    tn = 2048
    hf0, n_hf = hg_w // tn, hg_w // tn
    n_bf = w_in.shape[2] // tn - n_hf

    assert n_hf == 1
    h, pf = _ln_proj_call(x2, row(ln_in_g), row(ln_in_b), w_all, hf0, tn)
    pbf = _matmul_call(h, w_all, None, BF16, "proj_bf16", n_bf,
                       lambda j: j + jnp.where(j >= hf0, n_hf, 0), tn=tn)
    ya = _hgrn2_call(pbf, pf, hg_lb_logits, row(hg_norm_g[0]), batch, seq, hg_heads)

    idx = jnp.arange(SB_BLOCK)
    later = (idx[:, None] > idx[None, :]).astype(BF16)
    yb, gates = _sb_gates_call(pbf, later, h, (0.5 * w_merge[0]).astype(BF16),
                               row(0.5 * b_merge[0]), batch, seq, sb_heads, 3 * hg_w)

    out = _tail_call(alpha, ya, yb, gates, x2, p2, row(ln_in_g), row(ln_in_b),
                     w_br_hg[0].astype(BF16), w_br_sb[0].astype(BF16), (0.5 * w_out[0]).astype(BF16),
                     (0.5 * w_pg[0]).astype(BF16), row(0.5 * b_pg[0]), (0.5 * w_pe[0]).astype(BF16),
                     row(ln_g[0]), row(ln_b[0]))
    return out.reshape(batch, seq, d)
```

```python
import functools

import jax
import jax.numpy as jnp
from jax import lax
from jax.experimental import pallas as pl
from jax.experimental.pallas import tpu as pltpu

LN_EPS = 1e-5
LOG2_E = 1.4426950408889634
HEAD = 128
LN_SLAB = 256
HG_CHUNK = 128
HG_HEADS_PER_STEP = 4
HG_ROWS_PER_STEP = 1024
SB_BLOCK = 256
SB_EXP_ZERO = 104.0
SB_GATE_COLS = 2048
SB_GATE_CHUNK = 256
VMEM_LIMIT = 52 * 1024 * 1024

F32 = jnp.float32
BF16 = jnp.bfloat16


def _dot(a, b):
    return jnp.dot(a, b, preferred_element_type=F32)


def _dot_nt(a, b):
    return lax.dot_general(a, b, (((1,), (1,)), ((), ())), preferred_element_type=F32)


def _layer_norm(x, g, b):
    mu = jnp.mean(x, axis=-1, keepdims=True)
    xc = x - mu
    var = jnp.mean(xc * xc, axis=-1, keepdims=True)
    return xc * lax.rsqrt(var + LN_EPS) * g + b


def _silu_of_half(half):
    return half * (jnp.tanh(half) + 1.0)


def _ln_proj_kernel(x_ref, g_ref, b_ref, w_ref, h_ref, o_ref):
    for r0 in range(0, x_ref.shape[0], LN_SLAB):
        rs = slice(r0, r0 + LN_SLAB)
        h = _layer_norm(x_ref[rs, :], g_ref[...], b_ref[...]).astype(h_ref.dtype)
        h_ref[rs, :] = h
        o_ref[rs, :] = _dot(h, w_ref[...])


def _ln_proj_call(x, g, b, w, w_block, tn, tm=512):
    m, d = x.shape
    tm = min(tm, m)
    return pl.pallas_call(
        _ln_proj_kernel,
        grid=(m // tm,),
        in_specs=[pl.BlockSpec((tm, d), lambda i: (i, 0)),
                  pl.BlockSpec((1, d), lambda i: (0, 0)),
                  pl.BlockSpec((1, d), lambda i: (0, 0)),
                  pl.BlockSpec((d, tn), lambda i: (0, w_block), pipeline_mode=pl.Buffered(1))],
        out_specs=[pl.BlockSpec((tm, d), lambda i: (i, 0)),
                   pl.BlockSpec((tm, tn), lambda i: (i, 0))],
        out_shape=[jax.ShapeDtypeStruct((m, d), BF16),
                   jax.ShapeDtypeStruct((m, tn), F32)],
        compiler_params=pltpu.CompilerParams(
            dimension_semantics=("parallel",), vmem_limit_bytes=VMEM_LIMIT),
        name="ln_proj_f32",
    )(x, g, b, w)


def _matmul_kernel(h_ref, w_ref, o_ref):
    o_ref[...] = _dot(h_ref[...], w_ref[...]).astype(o_ref.dtype)


def _matmul_gate_kernel(h_ref, w_ref, b_ref, o_ref):
    o_ref[...] = jnp.tanh(_dot(h_ref[...], w_ref[...]) + b_ref[...]).astype(o_ref.dtype)


def _matmul_call(h, w, bias, out_dtype, name, n_blocks, w_col=lambda j: j, tm=1024, tn=1024):
    m, k = h.shape
    n = n_blocks * tn
    tm = min(tm, m)
    in_specs = [pl.BlockSpec((tm, k), lambda i, j: (i, 0)),
                pl.BlockSpec((k, tn), lambda i, j: (0, w_col(j)))]
    args = [h, w]
    kern = _matmul_kernel
    if bias is not None:
        in_specs.append(pl.BlockSpec((1, tn), lambda i, j: (0, j)))
        args.append(bias)
        kern = _matmul_gate_kernel
    return pl.pallas_call(
        kern,
        grid=(m // tm, n // tn),
        in_specs=in_specs,
        out_specs=pl.BlockSpec((tm, tn), lambda i, j: (i, j)),
        out_shape=jax.ShapeDtypeStruct((m, n), out_dtype),
        compiler_params=pltpu.CompilerParams(
            dimension_semantics=("parallel", "parallel"), vmem_limit_bytes=VMEM_LIMIT),
        name=name,
    )(*args)


def _hgrn2_kernel(q_ref, f_ref, i_ref, g_ref, lbl_ref, ng_ref, h_ref, w_ref, o_ref, p_ref, st_ref):
    c = HG_CHUNK
    n_chunks = q_ref.shape[0] // c
    n_heads = q_ref.shape[1] // HEAD

    @pl.when(pl.program_id(2) == 0)
    def _():
        st_ref[...] = jnp.zeros_like(st_ref)

    lbl = lbl_ref[...]
    e = jnp.exp(lbl - jnp.max(lbl, axis=0, keepdims=True))
    lb_all = e[0:1, :] / jnp.sum(e, axis=0, keepdims=True)

    row = lax.broadcasted_iota(jnp.int32, (c, c), 0)
    col = lax.broadcasted_iota(jnp.int32, (c, c), 1)
    causal = col <= row
    tril = jnp.where(causal, 1.0, 0.0).astype(BF16)
    tril2 = jnp.concatenate([tril, tril], axis=1)

    heads = [slice(hh * HEAD, (hh + 1) * HEAD) for hh in range(n_heads)]
    chunks = [slice(ci * c, (ci + 1) * c) for ci in range(n_chunks)]
    f_half = 0.5 * (1.0 - lb_all)
    f_mid = lb_all + f_half

    def decays(rs):
        f = f_mid + f_half * jnp.tanh(f_ref[rs, :])
        lf2 = jnp.log2(f)
        hi = lf2.astype(BF16)
        lo = (lf2 - hi.astype(F32)).astype(BF16)
        return 1.0 - f, _dot(tril2, jnp.concatenate([hi, lo], axis=0))

    def operands(rs, cs, kk, bc):
        b_last = bc[c - 1:c, :]
        b_mid = bc[c // 2 - 1:c // 2, :]
        q = q_ref[rs, cs].astype(F32)
        q_dec = (q * jnp.exp2(bc)).astype(BF16)
        q_n = (q * jnp.exp2(bc - b_mid)).astype(BF16)
        k_mid = kk * jnp.exp2(b_mid - bc)
        k_end = (k_mid * jnp.exp2(b_last - b_mid)).astype(BF16)
        a = jnp.where(causal, _dot_nt(q_n, k_mid.astype(BF16)), 0.0).astype(BF16)
        v_t = i_ref[rs, cs].astype(F32).T.astype(BF16)
        return a, q_dec, k_end, v_t, jnp.exp2(b_last)

    def recur(rs, cs, st, a, q_dec, k_end, v_t, dec):
        o = _dot_nt(jnp.concatenate([a, q_dec], axis=1),
                    jnp.concatenate([v_t, st.astype(BF16)], axis=1))
        ms = jnp.mean(o * o, axis=-1, keepdims=True)
        g = g_ref[rs, cs].astype(F32)
        y = o * lax.rsqrt(ms + LN_EPS) * ng_ref[:, cs] * _silu_of_half(g)
        o_ref[rs, cs] = y.astype(o_ref.dtype)
        return st * dec + _dot(v_t, k_end)

    states = [st_ref[hh] for hh in range(n_heads)]
    prep, ops = {}, {}
    proj_w = p_ref.shape[1] // n_chunks
    for step in range(n_chunks + 2):
        if step < n_chunks:
            prep[step] = decays(chunks[step])
            ps = slice(step * proj_w, (step + 1) * proj_w)
            p_ref[:, ps] = _dot(h_ref[...], w_ref[:, ps]).astype(p_ref.dtype)
        ci = step - 1
        if 0 <= ci < n_chunks:
            kk_all, bc_all = prep.pop(ci)
            for hh, cs in enumerate(heads):
                ops[ci, hh] = operands(chunks[ci], cs, kk_all[:, cs], bc_all[:, cs])
        ci = step - 2
        if 0 <= ci < n_chunks:
            for hh, cs in enumerate(heads):
                states[hh] = recur(chunks[ci], cs, states[hh], *ops.pop((ci, hh)))
    for hh in range(n_heads):
        st_ref[hh] = states[hh]


def _hgrn2_proj_call(pbf, pf, lb_logits, norm_g, h, w_all, w_block0, pn, batch, seq, n_heads):
    m = batch * seq
    hb, tt = HG_HEADS_PER_STEP, min(HG_ROWS_PER_STEP, seq)
    w = hb * HEAD
    nt = seq // tt
    n_hg = n_heads // hb
    sec = n_heads * HEAD // w
    d = h.shape[1]
    nj = w_all.shape[1] // pn - w_block0
    n_steps = batch * n_hg * nt
    pm = m * nj // n_steps
    assert pm * n_steps == m * nj and pm % 16 == 0 and pn % (tt // HG_CHUNK * HEAD) == 0

    def rows(b, h, t):
        return b * nt + t

    def step(b, h, t):
        return (b * n_hg + h) * nt + t

    return pl.pallas_call(
        _hgrn2_kernel,
        grid=(batch, n_hg, nt),
        in_specs=[pl.BlockSpec((tt, w), lambda b, h, t: (rows(b, h, t), h)),
                  pl.BlockSpec((tt, w), lambda b, h, t: (rows(b, h, t), h)),
                  pl.BlockSpec((tt, w), lambda b, h, t: (rows(b, h, t), sec + h)),
                  pl.BlockSpec((tt, w), lambda b, h, t: (rows(b, h, t), 2 * sec + h)),
                  pl.BlockSpec((2, w), lambda b, h, t: (0, h)),
                  pl.BlockSpec((1, w), lambda b, h, t: (0, h)),
                  pl.BlockSpec((pm, d), lambda b, h, t: (step(b, h, t) // nj, 0)),
                  pl.BlockSpec((d, pn), lambda b, h, t: (0, w_block0 + step(b, h, t) % nj))],
        out_specs=[pl.BlockSpec((tt, w), lambda b, h, t: (rows(b, h, t), h)),
                   pl.BlockSpec((pm, pn), lambda b, h, t: (step(b, h, t) // nj, step(b, h, t) % nj))],
        out_shape=[jax.ShapeDtypeStruct((m, n_heads * HEAD), BF16),
                   jax.ShapeDtypeStruct((m, nj * pn), BF16)],
        scratch_shapes=[pltpu.VMEM((hb, HEAD, HEAD), F32)],
        compiler_params=pltpu.CompilerParams(
            dimension_semantics=("parallel", "parallel", "arbitrary"),
            vmem_limit_bytes=VMEM_LIMIT),
        name="hgrn2_proj",
    )(pbf, pf, pbf, pbf, lb_logits, norm_g, h, w_all)


def _softplus(z):
    return jnp.maximum(z, 0.0) + jnp.log(1.0 + jnp.exp2(jnp.abs(z) * -LOG2_E))


def _sb_scores(q, k, diagonal):
    blk = q.shape[0]
    z = _dot_nt(q, k)
    sp = _softplus(z)
    log_beta = z - sp
    strict = None
    if diagonal:
        qrow = lax.broadcasted_iota(jnp.int32, (blk, blk), 0)
        kcol = lax.broadcasted_iota(jnp.int32, (blk, blk), 1)
        strict = kcol < qrow
        sp = jnp.where(strict, sp, 0.0)
    return log_beta, sp.astype(BF16), jnp.sum(sp, axis=-1, keepdims=True), strict


def _sb_weights(log_beta, sp_b, later, s, strict):
    blk = log_beta.shape[0]
    log_a = log_beta - _dot(sp_b, later)
    if s is not None:
        log_a = log_a - jnp.concatenate([s] * (blk // HEAD), axis=1)
    a = jnp.exp(log_a)
    if strict is not None:
        a = jnp.where(strict, a, 0.0)
    return a.astype(BF16)


def _sb_sweep(units, later, acc, s, side_work=()):
    n = len(units)
    st1, st2 = {}, {}
    side_work = list(side_work)
    stride = max(1, (n + 2) // max(1, len(side_work)))
    for t in range(n + 2):
        if side_work and t % stride == 0:
            side_work.pop(0)()
        if t < n:
            _, q, k, _, diagonal = units[t]
            st1[t] = _sb_scores(q, k, diagonal)
        u = t - 1
        if 0 <= u < n:
            h = units[u][0]
            log_beta, sp_b, rowsum, strict = st1.pop(u)
            st2[u] = _sb_weights(log_beta, sp_b, later, s.get(h), strict)
            s[h] = s[h] + rowsum if h in s else jnp.broadcast_to(rowsum, (rowsum.shape[0], HEAD))
        u = t - 2
        if 0 <= u < n:
            h, v = units[u][0], units[u][3]
            part = _dot(st2.pop(u), v)
            acc[h] = acc[h] + part if h in acc else part
    for work in side_work:
        work()


def _sb_kernel(k_col0, seq, q_ref, kd_ref, vd_ref, kp_ref, vp_ref, g_ref, later_ref, kv_hbm,
               h_ref, wm_ref, bm_ref, o_ref, gt_ref, acc_ref, s_ref, kbuf, vbuf, sem):
    blk = SB_BLOCK
    b = pl.program_id(0)
    qi = pl.program_id(1)
    width = q_ref.shape[1]
    n_heads = width // HEAD
    heads = [slice(h * HEAD, (h + 1) * HEAD) for h in range(n_heads)]
    later = later_ref[...]
    has_prev = qi > 0

    units = []
    for h, cs in enumerate(heads):
        q = q_ref[:, cs]
        units.append((h, q, kd_ref[:, cs], vd_ref[:, cs], True))
        vp = jnp.where(has_prev, vp_ref[:, cs], jnp.zeros((blk, HEAD), BF16))
        units.append((h, q, kp_ref[:, cs], vp, False))
    def gate_chunk(c0):
        def work():
            cs = slice(c0, c0 + SB_GATE_CHUNK)
            gate = _dot(h_ref[...], wm_ref[:, cs]) + bm_ref[:, cs]
            gt_ref[:, cs] = jnp.tanh(gate).astype(gt_ref.dtype)
        return work

    acc, s = {}, {}
    _sb_sweep(units, later, acc, s,
              [gate_chunk(c0) for c0 in range(0, gt_ref.shape[1], SB_GATE_CHUNK)])
    s_min = None
    for h in range(n_heads):
        acc_ref[h] = acc[h]
        s_ref[h] = s[h]
        s_min = s[h] if s_min is None else jnp.minimum(s_min, s[h])

    def cond(carry):
        kb, s_low = carry
        return jnp.logical_and(kb >= 0, s_low < SB_EXP_ZERO)

    def body(carry):
        kb, _ = carry
        row0 = pl.multiple_of(b * seq + kb * blk, blk)
        ck = pltpu.make_async_copy(
            kv_hbm.at[pl.ds(row0, blk), pl.ds(k_col0, width)], kbuf, sem.at[0])
        cv = pltpu.make_async_copy(
            kv_hbm.at[pl.ds(row0, blk), pl.ds(k_col0 + width, width)], vbuf, sem.at[1])
        ck.start()
        cv.start()
        ck.wait()
        cv.wait()
        units = [(h, q_ref[:, cs], kbuf[:, cs], vbuf[:, cs], False) for h, cs in enumerate(heads)]
        acc = {h: acc_ref[h] for h in range(n_heads)}
        s = {h: s_ref[h] for h in range(n_heads)}
        _sb_sweep(units, later, acc, s)
        s_low = None
        for h in range(n_heads):
            acc_ref[h] = acc[h]
            s_ref[h] = s[h]
            s_low = s[h] if s_low is None else jnp.minimum(s_low, s[h])
        return kb - 1, jnp.min(s_low)

    lax.while_loop(cond, body, (qi - 2, jnp.min(s_min)))

    for h, cs in enumerate(heads):
        g = g_ref[:, cs].astype(F32)
        o_ref[:, cs] = (acc_ref[h] * _silu_of_half(g)).astype(o_ref.dtype)


def _sb_gates_call(pbf, later, h, w_merge_half, b_merge_half, batch, seq, n_heads, col0):
    m = batch * seq
    blk = SB_BLOCK
    nq = seq // blk
    width = n_heads * HEAD
    c0 = col0 // width
    d, gw = w_merge_half.shape
    gn = SB_GATE_COLS
    nj = gw // gn
    gm = m * nj // (batch * nq)
    assert gm * batch * nq == m * nj and gm % 16 == 0 and gn % SB_GATE_CHUNK == 0

    def tile(sec):
        return pl.BlockSpec((blk, width), lambda b, i: (b * nq + i, c0 + sec))

    def prev(sec):
        return pl.BlockSpec((blk, width), lambda b, i: (b * nq + jnp.maximum(i - 1, 0), c0 + sec))

    def step(b, i):
        return b * nq + i

    return pl.pallas_call(
        functools.partial(_sb_kernel, col0 + width, seq),
        grid=(batch, nq),
        in_specs=[tile(0), tile(1), tile(2), prev(1), prev(2), tile(3),
                  pl.BlockSpec((blk, blk), lambda b, i: (0, 0)),
                  pl.BlockSpec(memory_space=pl.ANY),
                  pl.BlockSpec((gm, d), lambda b, i: (step(b, i) // nj, 0)),
                  pl.BlockSpec((d, gn), lambda b, i: (0, step(b, i) % nj)),
                  pl.BlockSpec((1, gn), lambda b, i: (0, step(b, i) % nj))],
        out_specs=[pl.BlockSpec((blk, width), lambda b, i: (b * nq + i, 0)),
                   pl.BlockSpec((gm, gn), lambda b, i: (step(b, i) // nj, step(b, i) % nj))],
        out_shape=[jax.ShapeDtypeStruct((m, width), BF16),
                   jax.ShapeDtypeStruct((m, gw), BF16)],
        scratch_shapes=[pltpu.VMEM((n_heads, blk, HEAD), F32),
                        pltpu.VMEM((n_heads, blk, HEAD), F32),
                        pltpu.VMEM((blk, width), BF16),
                        pltpu.VMEM((blk, width), BF16),
                        pltpu.SemaphoreType.DMA((2,))],
        compiler_params=pltpu.CompilerParams(
            dimension_semantics=("parallel", "arbitrary"),
            vmem_limit_bytes=VMEM_LIMIT),
        name="stick_breaking_gates",
    )(pbf, pbf, pbf, pbf, pbf, pbf, later, pbf, h, w_merge_half, b_merge_half)


def _tail_kernel(alpha, ya_ref, yb_ref, gt_ref, x_ref, p_ref, lig_ref, lib_ref,
                 wa_ref, wb_ref, wo_ref, wpg_ref, bpg_ref, wpe_ref, lg_ref, lb_ref, o_ref):
    d = o_ref.shape[1]
    ma = _dot(ya_ref[...], wa_ref[...])
    mb = _dot(yb_ref[...], wb_ref[...])
    mm = (gt_ref[:, :d].astype(F32) * ma + ma) + (gt_ref[:, d:].astype(F32) * mb + mb)
    h = _layer_norm(x_ref[...], lig_ref[...], lib_ref[...])
    r = alpha * h + _dot(mm.astype(BF16), wo_ref[...])
    tg = jnp.tanh(_dot(r.astype(BF16), wpg_ref[...]) + bpg_ref[...])
    pe = _dot(p_ref[...].astype(BF16), wpe_ref[...])
    r = r + (tg * pe + pe)
    o_ref[...] = _layer_norm(r, lg_ref[...], lb_ref[...])


def _tail_call(alpha, ya, yb, gates, x, p, lig, lib, wa, wb, wo, wpg, bpg, wpe, lg, lb, tm=256):
    m, d = x.shape
    tm = min(tm, m)

    def act(a):
        return pl.BlockSpec((tm, a.shape[1]), lambda i: (i, 0))

    def const(a):
        return pl.BlockSpec(a.shape, lambda i: (0, 0), pipeline_mode=pl.Buffered(1))

    return pl.pallas_call(
        functools.partial(_tail_kernel, alpha),
        grid=(m // tm,),
        in_specs=[act(ya), act(yb), act(gates), act(x), act(p), const(lig), const(lib),
                  const(wa), const(wb), const(wo), const(wpg), const(bpg), const(wpe),
                  const(lg), const(lb)],
        out_specs=pl.BlockSpec((tm, d), lambda i: (i, 0)),
        out_shape=jax.ShapeDtypeStruct((m, d), F32),
        compiler_params=pltpu.CompilerParams(
            dimension_semantics=("parallel",), vmem_limit_bytes=VMEM_LIMIT),
        name="tail",
    )(ya, yb, gates, x, p, lig, lib, wa, wb, wo, wpg, bpg, wpe, lg, lb)


def kernel(x, p, ln_in_g, ln_in_b, w_in, hg_lb_logits, hg_norm_g, w_merge, b_merge,
           w_br_hg, w_br_sb, w_out, w_pe, w_pg, b_pg, ln_g, ln_b):
    batch, seq, d = x.shape
    depth = w_in.shape[0]
    assert depth == 1, "kernel is written for a single layer"
    hg_w = w_br_hg.shape[1]
    sb_w = w_br_sb.shape[1]
    assert w_in.shape[2] == 4 * hg_w + 4 * sb_w
    hg_heads, sb_heads = hg_w // HEAD, sb_w // HEAD
    alpha = float((2 * depth) ** 0.25)
    m = batch * seq

    x2 = x.reshape(m, d)
    p2 = p[0].reshape(m, p.shape[-1])
    row = lambda a: a.reshape(1, -1)

    sec_scale = [1.0, 0.5, 1.0, 0.5] + [HEAD ** -0.5, 1.0, 1.0, 0.5]
    col_scale = jnp.concatenate(
        [jnp.full((hg_w,), sc, F32) for sc in sec_scale[:4]]
        + [jnp.full((sb_w,), sc, F32) for sc in sec_scale[4:]])
    w_all = (w_in[0] * col_scale).astype(BF16)
    tn = 2048
    hf0, n_hf = hg_w // tn, hg_w // tn

    assert n_hf == 1
    h, pf = _ln_proj_call(x2, row(ln_in_g), row(ln_in_b), w_all, hf0, tn)
    n_hgp = 3 * hg_w // tn
    pbf = _matmul_call(h, w_all, None, BF16, "proj_bf16", n_hgp,
                       lambda j: j + jnp.where(j >= hf0, n_hf, 0), tn=tn)
    ya, psb = _hgrn2_proj_call(pbf, pf, hg_lb_logits, row(hg_norm_g[0]), h, w_all,
                               4 * hg_w // tn, tn, batch, seq, hg_heads)

    idx = jnp.arange(SB_BLOCK)
    later = (idx[:, None] > idx[None, :]).astype(BF16)
    yb, gates = _sb_gates_call(psb, later, h, (0.5 * w_merge[0]).astype(BF16),
                               row(0.5 * b_merge[0]), batch, seq, sb_heads, 0)

    out = _tail_call(alpha, ya, yb, gates, x2, p2, row(ln_in_g), row(ln_in_b),
                     w_br_hg[0].astype(BF16), w_br_sb[0].astype(BF16), (0.5 * w_out[0]).astype(BF16),
                     (0.5 * w_pg[0]).astype(BF16), row(0.5 * b_pg[0]), (0.5 * w_pe[0]).astype(BF16),
                     row(ln_g[0]), row(ln_b[0]))
    return out.reshape(batch, seq, d)
```

```python
import functools

import jax
import jax.numpy as jnp
from jax import lax
from jax.experimental import pallas as pl
from jax.experimental.pallas import tpu as pltpu

LN_EPS = 1e-5
LOG2_E = 1.4426950408889634
HEAD = 128
LN_SLAB = 256
HG_CHUNK = 128
HG_HEADS_PER_STEP = 4
HG_ROWS_PER_STEP = 1024
SB_BLOCK = 256
SB_EXP_ZERO = 104.0
SB_GATE_COLS = 2048
SB_GATE_CHUNK = 256
VMEM_LIMIT = 52 * 1024 * 1024

F32 = jnp.float32
BF16 = jnp.bfloat16


def _dot(a, b):
    return jnp.dot(a, b, preferred_element_type=F32)


def _dot_nt(a, b):
    return lax.dot_general(a, b, (((1,), (1,)), ((), ())), preferred_element_type=F32)


def _layer_norm(x, g, b):
    mu = jnp.mean(x, axis=-1, keepdims=True)
    xc = x - mu
    var = jnp.mean(xc * xc, axis=-1, keepdims=True)
    return xc * lax.rsqrt(var + LN_EPS) * g + b


def _silu_of_half(half):
    return half * (jnp.tanh(half) + 1.0)


def _ln_proj_kernel(x_ref, g_ref, b_ref, w_ref, h_ref, o_ref):
    for r0 in range(0, x_ref.shape[0], LN_SLAB):
        rs = slice(r0, r0 + LN_SLAB)
        h = _layer_norm(x_ref[rs, :], g_ref[...], b_ref[...]).astype(h_ref.dtype)
        h_ref[rs, :] = h
        o_ref[rs, :] = _dot(h, w_ref[...])


def _ln_proj_call(x, g, b, w, w_block, tn, tm=512):
    m, d = x.shape
    tm = min(tm, m)
    return pl.pallas_call(
        _ln_proj_kernel,
        grid=(m // tm,),
        in_specs=[pl.BlockSpec((tm, d), lambda i: (i, 0)),
                  pl.BlockSpec((1, d), lambda i: (0, 0)),
                  pl.BlockSpec((1, d), lambda i: (0, 0)),
                  pl.BlockSpec((d, tn), lambda i: (0, w_block), pipeline_mode=pl.Buffered(1))],
        out_specs=[pl.BlockSpec((tm, d), lambda i: (i, 0)),
                   pl.BlockSpec((tm, tn), lambda i: (i, 0))],
        out_shape=[jax.ShapeDtypeStruct((m, d), BF16),
                   jax.ShapeDtypeStruct((m, tn), F32)],
        compiler_params=pltpu.CompilerParams(
            dimension_semantics=("parallel",), vmem_limit_bytes=VMEM_LIMIT),
        name="ln_proj_f32",
    )(x, g, b, w)


def _matmul_kernel(h_ref, w_ref, o_ref):
    o_ref[...] = _dot(h_ref[...], w_ref[...]).astype(o_ref.dtype)


def _matmul_gate_kernel(h_ref, w_ref, b_ref, o_ref):
    o_ref[...] = jnp.tanh(_dot(h_ref[...], w_ref[...]) + b_ref[...]).astype(o_ref.dtype)


def _matmul_call(h, w, bias, out_dtype, name, n_blocks, w_col=lambda j: j, tm=1024, tn=1024):
    m, k = h.shape
    n = n_blocks * tn
    tm = min(tm, m)
    in_specs = [pl.BlockSpec((tm, k), lambda j, i: (i, 0)),
                pl.BlockSpec((k, tn), lambda j, i: (0, w_col(j)))]
    args = [h, w]
    kern = _matmul_kernel
    if bias is not None:
        in_specs.append(pl.BlockSpec((1, tn), lambda j, i: (0, j)))
        args.append(bias)
        kern = _matmul_gate_kernel
    return pl.pallas_call(
        kern,
        grid=(n // tn, m // tm),
        in_specs=in_specs,
        out_specs=pl.BlockSpec((tm, tn), lambda j, i: (i, j)),
        out_shape=jax.ShapeDtypeStruct((m, n), out_dtype),
        compiler_params=pltpu.CompilerParams(
            dimension_semantics=("parallel", "parallel"), vmem_limit_bytes=VMEM_LIMIT),
        name=name,
    )(*args)


def _hgrn2_kernel(q_ref, f_ref, i_ref, g_ref, lbl_ref, ng_ref, h_ref, w_ref, o_ref, p_ref, st_ref):
    c = HG_CHUNK
    n_chunks = q_ref.shape[0] // c
    n_heads = q_ref.shape[1] // HEAD

    @pl.when(pl.program_id(2) == 0)
    def _():
        st_ref[...] = jnp.zeros_like(st_ref)

    lbl = lbl_ref[...]
    e = jnp.exp(lbl - jnp.max(lbl, axis=0, keepdims=True))
    lb_all = e[0:1, :] / jnp.sum(e, axis=0, keepdims=True)

    row = lax.broadcasted_iota(jnp.int32, (c, c), 0)
    col = lax.broadcasted_iota(jnp.int32, (c, c), 1)
    causal = col <= row
    tril = jnp.where(causal, 1.0, 0.0).astype(BF16)
    tril2 = jnp.concatenate([tril, tril], axis=1)

    heads = [slice(hh * HEAD, (hh + 1) * HEAD) for hh in range(n_heads)]
    chunks = [slice(ci * c, (ci + 1) * c) for ci in range(n_chunks)]
    f_half = 0.5 * (1.0 - lb_all)
    f_mid = lb_all + f_half

    def decays(rs):
        f = f_mid + f_half * jnp.tanh(f_ref[rs, :])
        lf2 = jnp.log2(f)
        hi = lf2.astype(BF16)
        lo = (lf2 - hi.astype(F32)).astype(BF16)
        return 1.0 - f, _dot(tril2, jnp.concatenate([hi, lo], axis=0))

    def operands(rs, cs, kk, bc):
        b_last = bc[c - 1:c, :]
        b_mid = bc[c // 2 - 1:c // 2, :]
        q = q_ref[rs, cs].astype(F32)
        q_dec = (q * jnp.exp2(bc)).astype(BF16)
        q_n = (q * jnp.exp2(bc - b_mid)).astype(BF16)
        k_mid = kk * jnp.exp2(b_mid - bc)
        k_end = (k_mid * jnp.exp2(b_last - b_mid)).astype(BF16)
        a = jnp.where(causal, _dot_nt(q_n, k_mid.astype(BF16)), 0.0).astype(BF16)
        v_t = i_ref[rs, cs].astype(F32).T.astype(BF16)
        return a, q_dec, k_end, v_t, jnp.exp2(b_last)

    def recur(rs, cs, st, a, q_dec, k_end, v_t, dec):
        o = _dot_nt(jnp.concatenate([a, q_dec], axis=1),
                    jnp.concatenate([v_t, st.astype(BF16)], axis=1))
        ms = jnp.mean(o * o, axis=-1, keepdims=True)
        g = g_ref[rs, cs].astype(F32)
        y = o * lax.rsqrt(ms + LN_EPS) * ng_ref[:, cs] * _silu_of_half(g)
        o_ref[rs, cs] = y.astype(o_ref.dtype)
        return st * dec + _dot(v_t, k_end)

    states = [st_ref[hh] for hh in range(n_heads)]
    prep, ops = {}, {}
    proj_w = p_ref.shape[1] // n_chunks
    for step in range(n_chunks + 2):
        if step < n_chunks:
            prep[step] = decays(chunks[step])
            ps = slice(step * proj_w, (step + 1) * proj_w)
            p_ref[:, ps] = _dot(h_ref[...], w_ref[:, ps]).astype(p_ref.dtype)
        ci = step - 1
        if 0 <= ci < n_chunks:
            kk_all, bc_all = prep.pop(ci)
            for hh, cs in enumerate(heads):
                ops[ci, hh] = operands(chunks[ci], cs, kk_all[:, cs], bc_all[:, cs])
        ci = step - 2
        if 0 <= ci < n_chunks:
            for hh, cs in enumerate(heads):
                states[hh] = recur(chunks[ci], cs, states[hh], *ops.pop((ci, hh)))
    for hh in range(n_heads):
        st_ref[hh] = states[hh]


def _hgrn2_proj_call(pbf, pf, lb_logits, norm_g, h, w_all, w_block0, pn, batch, seq, n_heads):
    m = batch * seq
    hb, tt = HG_HEADS_PER_STEP, min(HG_ROWS_PER_STEP, seq)
    w = hb * HEAD
    nt = seq // tt
    n_hg = n_heads // hb
    sec = n_heads * HEAD // w
    d = h.shape[1]
    nj = w_all.shape[1] // pn - w_block0
    n_steps = batch * n_hg * nt
    pm = m * nj // n_steps
    assert pm * n_steps == m * nj and pm % 16 == 0 and pn % (tt // HG_CHUNK * HEAD) == 0

    def rows(b, h, t):
        return b * nt + t

    def step(b, h, t):
        return (b * n_hg + h) * nt + t

    return pl.pallas_call(
        _hgrn2_kernel,
        grid=(batch, n_hg, nt),
        in_specs=[pl.BlockSpec((tt, w), lambda b, h, t: (rows(b, h, t), h)),
                  pl.BlockSpec((tt, w), lambda b, h, t: (rows(b, h, t), h)),
                  pl.BlockSpec((tt, w), lambda b, h, t: (rows(b, h, t), sec + h)),
                  pl.BlockSpec((tt, w), lambda b, h, t: (rows(b, h, t), 2 * sec + h)),
                  pl.BlockSpec((2, w), lambda b, h, t: (0, h)),
                  pl.BlockSpec((1, w), lambda b, h, t: (0, h)),
                  pl.BlockSpec((pm, d), lambda b, h, t: (step(b, h, t) // nj, 0)),
                  pl.BlockSpec((d, pn), lambda b, h, t: (0, w_block0 + step(b, h, t) % nj))],
        out_specs=[pl.BlockSpec((tt, w), lambda b, h, t: (rows(b, h, t), h)),
                   pl.BlockSpec((pm, pn), lambda b, h, t: (step(b, h, t) // nj, step(b, h, t) % nj))],
        out_shape=[jax.ShapeDtypeStruct((m, n_heads * HEAD), BF16),
                   jax.ShapeDtypeStruct((m, nj * pn), BF16)],
        scratch_shapes=[pltpu.VMEM((hb, HEAD, HEAD), F32)],
        compiler_params=pltpu.CompilerParams(
            dimension_semantics=("parallel", "parallel", "arbitrary"),
            vmem_limit_bytes=VMEM_LIMIT),
        name="hgrn2_proj",
    )(pbf, pf, pbf, pbf, lb_logits, norm_g, h, w_all)


def _softplus(z):
    return jnp.maximum(z, 0.0) + jnp.log(1.0 + jnp.exp2(jnp.abs(z) * -LOG2_E))


def _sb_scores(q, k, diagonal):
    blk = q.shape[0]
    z = _dot_nt(q, k)
    sp = _softplus(z)
    log_beta = z - sp
    strict = None
    if diagonal:
        qrow = lax.broadcasted_iota(jnp.int32, (blk, blk), 0)
        kcol = lax.broadcasted_iota(jnp.int32, (blk, blk), 1)
        strict = kcol < qrow
        sp = jnp.where(strict, sp, 0.0)
    return log_beta, sp.astype(BF16), jnp.sum(sp, axis=-1, keepdims=True), strict


def _sb_weights(log_beta, sp_b, later, s, strict):
    blk = log_beta.shape[0]
    log_a = log_beta - _dot(sp_b, later)
    if s is not None:
        log_a = log_a - jnp.concatenate([s] * (blk // HEAD), axis=1)
    a = jnp.exp(log_a)
    if strict is not None:
        a = jnp.where(strict, a, 0.0)
    return a.astype(BF16)


def _sb_sweep(units, later, acc, s, side_work=()):
    n = len(units)
    st1, st2 = {}, {}
    side_work = list(side_work)
    stride = max(1, (n + 2) // max(1, len(side_work)))
    for t in range(n + 2):
        if side_work and t % stride == 0:
            side_work.pop(0)()
        if t < n:
            _, q, k, _, diagonal = units[t]
            st1[t] = _sb_scores(q, k, diagonal)
        u = t - 1
        if 0 <= u < n:
            h = units[u][0]
            log_beta, sp_b, rowsum, strict = st1.pop(u)
            st2[u] = _sb_weights(log_beta, sp_b, later, s.get(h), strict)
            s[h] = s[h] + rowsum if h in s else jnp.broadcast_to(rowsum, (rowsum.shape[0], HEAD))
        u = t - 2
        if 0 <= u < n:
            h, v = units[u][0], units[u][3]
            part = _dot(st2.pop(u), v)
            acc[h] = acc[h] + part if h in acc else part
    for work in side_work:
        work()


def _sb_kernel(k_col0, seq, q_ref, kd_ref, vd_ref, kp_ref, vp_ref, g_ref, later_ref, kv_hbm,
               h_ref, wm_ref, bm_ref, o_ref, gt_ref, acc_ref, s_ref, kbuf, vbuf, sem):
    blk = SB_BLOCK
    b = pl.program_id(0)
    qi = pl.program_id(1)
    width = q_ref.shape[1]
    n_heads = width // HEAD
    heads = [slice(h * HEAD, (h + 1) * HEAD) for h in range(n_heads)]
    later = later_ref[...]
    has_prev = qi > 0

    units = []
    for h, cs in enumerate(heads):
        q = q_ref[:, cs]
        units.append((h, q, kd_ref[:, cs], vd_ref[:, cs], True))
        vp = jnp.where(has_prev, vp_ref[:, cs], jnp.zeros((blk, HEAD), BF16))
        units.append((h, q, kp_ref[:, cs], vp, False))
    def gate_chunk(c0):
        def work():
            cs = slice(c0, c0 + SB_GATE_CHUNK)
            gate = _dot(h_ref[...], wm_ref[:, cs]) + bm_ref[:, cs]
            gt_ref[:, cs] = jnp.tanh(gate).astype(gt_ref.dtype)
        return work

    acc, s = {}, {}
    _sb_sweep(units, later, acc, s,
              [gate_chunk(c0) for c0 in range(0, gt_ref.shape[1], SB_GATE_CHUNK)])
    s_min = None
    for h in range(n_heads):
        acc_ref[h] = acc[h]
        s_ref[h] = s[h]
        s_min = s[h] if s_min is None else jnp.minimum(s_min, s[h])

    def cond(carry):
        kb, s_low = carry
        return jnp.logical_and(kb >= 0, s_low < SB_EXP_ZERO)

    def body(carry):
        kb, _ = carry
        row0 = pl.multiple_of(b * seq + kb * blk, blk)
        ck = pltpu.make_async_copy(
            kv_hbm.at[pl.ds(row0, blk), pl.ds(k_col0, width)], kbuf, sem.at[0])
        cv = pltpu.make_async_copy(
            kv_hbm.at[pl.ds(row0, blk), pl.ds(k_col0 + width, width)], vbuf, sem.at[1])
        ck.start()
        cv.start()
        ck.wait()
        cv.wait()
        units = [(h, q_ref[:, cs], kbuf[:, cs], vbuf[:, cs], False) for h, cs in enumerate(heads)]
        acc = {h: acc_ref[h] for h in range(n_heads)}
        s = {h: s_ref[h] for h in range(n_heads)}
        _sb_sweep(units, later, acc, s)
        s_low = None
        for h in range(n_heads):
            acc_ref[h] = acc[h]
            s_ref[h] = s[h]
            s_low = s[h] if s_low is None else jnp.minimum(s_low, s[h])
        return kb - 1, jnp.min(s_low)

    lax.while_loop(cond, body, (qi - 2, jnp.min(s_min)))

    for h, cs in enumerate(heads):
        g = g_ref[:, cs].astype(F32)
        o_ref[:, cs] = (acc_ref[h] * _silu_of_half(g)).astype(o_ref.dtype)


def _sb_gates_call(pbf, later, h, w_merge_half, b_merge_half, batch, seq, n_heads, col0):
    m = batch * seq
    blk = SB_BLOCK
    nq = seq // blk
    width = n_heads * HEAD
    c0 = col0 // width
    d, gw = w_merge_half.shape
    gn = SB_GATE_COLS
    nj = gw // gn
    gm = m * nj // (batch * nq)
    assert gm * batch * nq == m * nj and gm % 16 == 0 and gn % SB_GATE_CHUNK == 0

    def tile(sec):
        return pl.BlockSpec((blk, width), lambda b, i: (b * nq + i, c0 + sec))

    def prev(sec):
        return pl.BlockSpec((blk, width), lambda b, i: (b * nq + jnp.maximum(i - 1, 0), c0 + sec))

    def step(b, i):
        return b * nq + i

    return pl.pallas_call(
        functools.partial(_sb_kernel, col0 + width, seq),
        grid=(batch, nq),
        in_specs=[tile(0), tile(1), tile(2), prev(1), prev(2), tile(3),
                  pl.BlockSpec((blk, blk), lambda b, i: (0, 0)),
                  pl.BlockSpec(memory_space=pl.ANY),
                  pl.BlockSpec((gm, d), lambda b, i: (step(b, i) // nj, 0)),
                  pl.BlockSpec((d, gn), lambda b, i: (0, step(b, i) % nj)),
                  pl.BlockSpec((1, gn), lambda b, i: (0, step(b, i) % nj))],
        out_specs=[pl.BlockSpec((blk, width), lambda b, i: (b * nq + i, 0)),
                   pl.BlockSpec((gm, gn), lambda b, i: (step(b, i) // nj, step(b, i) % nj))],
        out_shape=[jax.ShapeDtypeStruct((m, width), BF16),
                   jax.ShapeDtypeStruct((m, gw), BF16)],
        scratch_shapes=[pltpu.VMEM((n_heads, blk, HEAD), F32),
                        pltpu.VMEM((n_heads, blk, HEAD), F32),
                        pltpu.VMEM((blk, width), BF16),
                        pltpu.VMEM((blk, width), BF16),
                        pltpu.SemaphoreType.DMA((2,))],
        compiler_params=pltpu.CompilerParams(
            dimension_semantics=("parallel", "arbitrary"),
            vmem_limit_bytes=VMEM_LIMIT),
        name="stick_breaking_gates",
    )(pbf, pbf, pbf, pbf, pbf, pbf, later, pbf, h, w_merge_half, b_merge_half)


def _tail_kernel(alpha, ya_ref, yb_ref, gt_ref, x_ref, p_ref, lig_ref, lib_ref,
                 wa_ref, wb_ref, wo_ref, wpg_ref, bpg_ref, wpe_ref, lg_ref, lb_ref, o_ref):
    d = o_ref.shape[1]
    ma = _dot(ya_ref[...], wa_ref[...])
    mb = _dot(yb_ref[...], wb_ref[...])
    mm = (gt_ref[:, :d].astype(F32) * ma + ma) + (gt_ref[:, d:].astype(F32) * mb + mb)
    h = _layer_norm(x_ref[...], lig_ref[...], lib_ref[...])
    r = alpha * h + _dot(mm.astype(BF16), wo_ref[...])
    tg = jnp.tanh(_dot(r.astype(BF16), wpg_ref[...]) + bpg_ref[...])
    pe = _dot(p_ref[...].astype(BF16), wpe_ref[...])
    r = r + (tg * pe + pe)
    o_ref[...] = _layer_norm(r, lg_ref[...], lb_ref[...])


def _tail_call(alpha, ya, yb, gates, x, p, lig, lib, wa, wb, wo, wpg, bpg, wpe, lg, lb, tm=256):
    m, d = x.shape
    tm = min(tm, m)

    def act(a):
        return pl.BlockSpec((tm, a.shape[1]), lambda i: (i, 0))

    def const(a):
        return pl.BlockSpec(a.shape, lambda i: (0, 0), pipeline_mode=pl.Buffered(1))

    return pl.pallas_call(
        functools.partial(_tail_kernel, alpha),
        grid=(m // tm,),
        in_specs=[act(ya), act(yb), act(gates), act(x), act(p), const(lig), const(lib),
                  const(wa), const(wb), const(wo), const(wpg), const(bpg), const(wpe),
                  const(lg), const(lb)],
        out_specs=pl.BlockSpec((tm, d), lambda i: (i, 0)),
        out_shape=jax.ShapeDtypeStruct((m, d), F32),
        compiler_params=pltpu.CompilerParams(
            dimension_semantics=("parallel",), vmem_limit_bytes=VMEM_LIMIT),
        name="tail",
    )(ya, yb, gates, x, p, lig, lib, wa, wb, wo, wpg, bpg, wpe, lg, lb)


def kernel(x, p, ln_in_g, ln_in_b, w_in, hg_lb_logits, hg_norm_g, w_merge, b_merge,
           w_br_hg, w_br_sb, w_out, w_pe, w_pg, b_pg, ln_g, ln_b):
    batch, seq, d = x.shape
    depth = w_in.shape[0]
    assert depth == 1, "kernel is written for a single layer"
    hg_w = w_br_hg.shape[1]
    sb_w = w_br_sb.shape[1]
    assert w_in.shape[2] == 4 * hg_w + 4 * sb_w
    hg_heads, sb_heads = hg_w // HEAD, sb_w // HEAD
    alpha = float((2 * depth) ** 0.25)
    m = batch * seq

    x2 = x.reshape(m, d)
    p2 = p[0].reshape(m, p.shape[-1])
    row = lambda a: a.reshape(1, -1)

    sec_scale = [1.0, 0.5, 1.0, 0.5] + [HEAD ** -0.5, 1.0, 1.0, 0.5]
    col_scale = jnp.concatenate(
        [jnp.full((hg_w,), sc, F32) for sc in sec_scale[:4]]
        + [jnp.full((sb_w,), sc, F32) for sc in sec_scale[4:]])
    w_all = (w_in[0] * col_scale).astype(BF16)
    tn = 2048
    hf0, n_hf = hg_w // tn, hg_w // tn

    assert n_hf == 1
    h, pf = _ln_proj_call(x2, row(ln_in_g), row(ln_in_b), w_all, hf0, tn)
    n_hgp = 3 * hg_w // tn
    pbf = _matmul_call(h, w_all, None, BF16, "proj_bf16", n_hgp,
                       lambda j: j + jnp.where(j >= hf0, n_hf, 0), tn=tn)
    ya, psb = _hgrn2_proj_call(pbf, pf, hg_lb_logits, row(hg_norm_g[0]), h, w_all,
                               4 * hg_w // tn, tn, batch, seq, hg_heads)

    idx = jnp.arange(SB_BLOCK)
    later = (idx[:, None] > idx[None, :]).astype(BF16)
    yb, gates = _sb_gates_call(psb, later, h, (0.5 * w_merge[0]).astype(BF16),
                               row(0.5 * b_merge[0]), batch, seq, sb_heads, 0)

    out = _tail_call(alpha, ya, yb, gates, x2, p2, row(ln_in_g), row(ln_in_b),
                     w_br_hg[0].astype(BF16), w_br_sb[0].astype(BF16), (0.5 * w_out[0]).astype(BF16),
                     (0.5 * w_pg[0]).astype(BF16), row(0.5 * b_pg[0]), (0.5 * w_pe[0]).astype(BF16),
                     row(ln_g[0]), row(ln_b[0]))
    return out.reshape(batch, seq, d)
```
